```python
import math
import jax, jax.numpy as jnp
from jax import lax
import numpy as np

D_MODEL = 1024
BATCH = 8
SEQ = 8192
DEPTH = 1

HEAD_DIM = 64
DIL_PATTERNS = ((128, 1), (512, 4), (2048, 16))
N_DIL_GROUPS = len(DIL_PATTERNS)
HEADS_PER_GROUP = 4
N_ATTN_HEADS = N_DIL_GROUPS * HEADS_PER_GROUP
ATTN_WIDTH = N_ATTN_HEADS * HEAD_DIM
ATTN_OUT_WIDTH = HEADS_PER_GROUP * HEAD_DIM
QBLK = 128
ROPE_THETA = 10000.0

CHUNK = 128
GMLP_GROUPS = 4
GMLP_GROUP_CH = 128
GMLP_WIDTH = GMLP_GROUPS * GMLP_GROUP_CH

D_FF = 4 * D_MODEL
EPS = 1e-6

Q0 = 0
K0 = Q0 + ATTN_WIDTH
V0 = K0 + ATTN_WIDTH
U0 = V0 + ATTN_WIDTH
Z0 = U0 + GMLP_WIDTH
GA0 = Z0 + GMLP_WIDTH
GB0 = GA0 + D_MODEL
IN_WIDTH = GB0 + D_MODEL

kernel_name = "hybrid_dilated_attn_gmlp_block"


def _rmsnorm(x, gain):
    xf = x.astype(jnp.float32)
    y = xf * lax.rsqrt(jnp.mean(xf * xf, axis=-1, keepdims=True) + EPS)
    return (y * gain.astype(jnp.float32)).astype(x.dtype)


def _layernorm(x, gain, bias):
    xf = x.astype(jnp.float32)
    mu = jnp.mean(xf, axis=-1, keepdims=True)
    var = jnp.mean(jnp.square(xf - mu), axis=-1, keepdims=True)
    y = (xf - mu) * lax.rsqrt(var + EPS)
    return (y * gain.astype(jnp.float32) + bias.astype(jnp.float32)).astype(x.dtype)


def _rope(x):
    S, Dh = x.shape[1], x.shape[-1]
    half = Dh // 2
    inv_freq = ROPE_THETA ** (-jnp.arange(half, dtype=jnp.float32) / half)
    ang = jnp.arange(S, dtype=jnp.float32)[:, None] * inv_freq[None, :]
    cos = jnp.cos(ang)[None, :, None, :]
    sin = jnp.sin(ang)[None, :, None, :]
    xf = x.astype(jnp.float32)
    x1, x2 = xf[..., :half], xf[..., half:]
    return jnp.concatenate([x1 * cos - x2 * sin, x2 * cos + x1 * sin], axis=-1).astype(x.dtype)


def _dilated_window_attention(q, k, v, dilation, n_back):
    B, S, H, Dh = q.shape
    L = S // dilation
    nb = -(-L // QBLK)
    Lp = nb * QBLK

    def to_sub(t):
        t = t.reshape(B, L, dilation, H, Dh).transpose(0, 2, 3, 1, 4)
        t = jnp.pad(t, ((0, 0), (0, 0), (0, 0), (0, Lp - L), (0, 0)))
        return t.reshape(B, dilation, H, nb, QBLK, Dh)

    def with_prev(t):
        prev = jnp.pad(t[:, :, :, :-1], ((0, 0), (0, 0), (0, 0), (1, 0), (0, 0), (0, 0)))
        return jnp.concatenate([prev, t], axis=4)

    qb, kb, vb = to_sub(q), to_sub(k), to_sub(v)
    kc, vc = with_prev(kb), with_prev(vb)
    s = jnp.einsum('brhnqe,brhnke->brhnqk', qb.astype(jnp.float32),
                   kc.astype(jnp.float32)) * (Dh ** -0.5)
    qi = jnp.arange(QBLK)[:, None]
    kj = jnp.arange(2 * QBLK)[None, :]
    dist = qi + QBLK - kj
    band = (dist >= 0) & (dist <= n_back)
    key_sub = jnp.arange(nb)[:, None, None] * QBLK + kj[None] - QBLK
    mask = band[None] & (key_sub >= 0)
    s = jnp.where(mask, s, jnp.float32(-1e30))
    m = jnp.max(s, axis=-1, keepdims=True)
    p = jnp.exp(s - m)
    den = jnp.sum(p, axis=-1, keepdims=True)
    o = jnp.einsum('brhnqk,brhnke->brhnqe', p, vc.astype(jnp.float32)) / den
    lse = (m + jnp.log(den))[..., 0]
    o = o.reshape(B, dilation, H, Lp, Dh)[:, :, :, :L].transpose(0, 3, 1, 2, 4).reshape(B, S, H, Dh)
    lse = lse.reshape(B, dilation, H, Lp)[:, :, :, :L].transpose(0, 3, 1, 2).reshape(B, S, H)
    return o, lse


def _mixer_dilated_attention(q, k, v):
    B, S = q.shape[0], q.shape[1]
    outs, lses = [], []
    for g, (window, dilation) in enumerate(DIL_PATTERNS):
        sl = slice(g * HEADS_PER_GROUP, (g + 1) * HEADS_PER_GROUP)
        o, lse = _dilated_window_attention(q[:, :, sl], k[:, :, sl], v[:, :, sl],
                                           dilation, window // dilation)
        outs.append(o)
        lses.append(lse)
    alpha = jax.nn.softmax(jnp.stack(lses, axis=0), axis=0)
    o = jnp.sum(alpha[..., None] * jnp.stack(outs, axis=0), axis=0)
    return o.reshape(B, S, ATTN_OUT_WIDTH).astype(q.dtype)


def _mixer_chunked_gmlp(u, z, ln_gain, ln_bias, w_spatial, b_spatial):
    B, S, _ = u.shape
    nc = S // CHUNK
    z = _layernorm(z, ln_gain, ln_bias)
    zc = z.reshape(B, nc, CHUNK, GMLP_GROUPS, GMLP_GROUP_CH)
    tril = jnp.tril(jnp.ones((CHUNK, CHUNK), dtype=bool))
    w = jnp.where(tril[None], w_spatial, jnp.zeros_like(w_spatial))
    sz = jnp.einsum('gij,bcjgd->bcigd', w.astype(jnp.float32), zc.astype(jnp.float32))
    sz = sz + b_spatial.T.astype(jnp.float32)[None, None, :, :, None]
    out = u.reshape(B, nc, CHUNK, GMLP_GROUPS, GMLP_GROUP_CH).astype(jnp.float32) * sz
    return out.reshape(B, S, GMLP_WIDTH).astype(u.dtype)


def setup_inputs(seed: int = 0) -> dict:
    key = jax.random.key(seed)
    ks = jax.random.split(key, 16)
    f32 = jnp.float32

    def dense(k, fan_in, fan_out):
        return jax.random.normal(k, (DEPTH, fan_in, fan_out), f32) * (fan_in ** -0.5)

    def gain(k, n):
        return 1.0 + 0.02 * jax.random.normal(k, (DEPTH, n), f32)

    return {
        "x": jax.random.normal(ks[0], (BATCH, SEQ, D_MODEL), f32),
        "norm_pre_mix": gain(ks[1], D_MODEL),
        "w_in": dense(ks[2], D_MODEL, IN_WIDTH),
        "w_spatial": jax.random.normal(ks[3], (DEPTH, GMLP_GROUPS, CHUNK, CHUNK), f32) * (CHUNK ** -0.5),
        "b_spatial": 1.0 + 0.1 * jax.random.normal(ks[4], (DEPTH, GMLP_GROUPS, CHUNK), f32),
        "ln_v_gain": gain(ks[5], GMLP_WIDTH),
        "ln_v_bias": 0.02 * jax.random.normal(ks[6], (DEPTH, GMLP_WIDTH), f32),
        "w_branch_attn": dense(ks[7], ATTN_OUT_WIDTH, D_MODEL),
        "w_branch_gmlp": dense(ks[8], GMLP_WIDTH, D_MODEL),
        "w_out": dense(ks[9], D_MODEL, D_MODEL),
        "norm_post_mix": gain(ks[10], D_MODEL),
        "norm_pre_mlp": gain(ks[11], D_MODEL),
        "w_mlp_in": dense(ks[12], D_MODEL, D_FF),
        "w_mlp_out": dense(ks[13], D_FF, D_MODEL),
        "norm_post_mlp": gain(ks[14], D_MODEL),
    }


def reference(x, norm_pre_mix, w_in, w_spatial, b_spatial, ln_v_gain, ln_v_bias,
              w_branch_attn, w_branch_gmlp, w_out, norm_post_mix, norm_pre_mlp,
              w_mlp_in, w_mlp_out, norm_post_mlp):
    B, S, D = x.shape
    for layer in range(DEPTH):
        h = _rmsnorm(x, norm_pre_mix[layer])
        proj = jnp.einsum('bsd,de->bse', h, w_in[layer])
        q = _rope(proj[..., Q0:K0].reshape(B, S, N_ATTN_HEADS, HEAD_DIM))
        k = _rope(proj[..., K0:V0].reshape(B, S, N_ATTN_HEADS, HEAD_DIM))
        v = proj[..., V0:U0].reshape(B, S, N_ATTN_HEADS, HEAD_DIM)
        u = jax.nn.gelu(proj[..., U0:Z0])
        z = jax.nn.gelu(proj[..., Z0:GA0])
        gate_a = jax.nn.sigmoid(proj[..., GA0:GB0])
        gate_b = jax.nn.sigmoid(proj[..., GB0:IN_WIDTH])

        y_attn = _mixer_dilated_attention(q, k, v)
        y_gmlp = _mixer_chunked_gmlp(u, z, ln_v_gain[layer], ln_v_bias[layer],
                                     w_spatial[layer], b_spatial[layer])
        merged = (gate_a * jnp.einsum('bse,ed->bsd', y_attn, w_branch_attn[layer])
                  + gate_b * jnp.einsum('bse,ed->bsd', y_gmlp, w_branch_gmlp[layer]))
        y = jnp.einsum('bsd,de->bse', merged, w_out[layer])
        x = x + _rmsnorm(y, norm_post_mix[layer])

        h = _rmsnorm(x, norm_pre_mlp[layer])
        a = jax.nn.relu(jnp.einsum('bsd,df->bsf', h, w_mlp_in[layer]))
        y = jnp.einsum('bsf,fd->bsd', a * a, w_mlp_out[layer])
        x = x + _rmsnorm(y, norm_post_mlp[layer])
    return x
```

```python
import functools

import numpy as np
import jax
import jax.numpy as jnp
from jax import lax
from jax.experimental import pallas as pl
from jax.experimental.pallas import tpu as pltpu

HEAD_DIM = 64
HALF = HEAD_DIM // 2
DIL_PATTERNS = ((128, 1), (512, 4), (2048, 16))
DILATIONS = tuple(d for _, d in DIL_PATTERNS)
N_GROUPS = len(DIL_PATTERNS)
HEADS_PER_GROUP = 4
GROUP_W = HEADS_PER_GROUP * HEAD_DIM
ATTN_WIDTH = N_GROUPS * GROUP_W
QBLK = 128
ROPE_THETA = 10000.0
CHUNK = 128
GMLP_GROUPS = 4
GMLP_WIDTH = GMLP_GROUPS * 128
EPS = 1e-6
NEG = -1e30

LANES = 128
N_SLABS = GROUP_W // LANES

BF16 = jnp.bfloat16
F32 = jnp.float32

VMEM_LIMIT_BYTES = 56 * 1024 * 1024


def _rms(y):
    return y * lax.rsqrt(jnp.mean(y * y, axis=-1, keepdims=True) + EPS)


def _gelu_tanh(x):
    c = np.float32(np.sqrt(2.0 / np.pi))
    return x * (0.5 * (1.0 + jnp.tanh(c * (x + np.float32(0.044715) * (x * x * x)))))


def _sigmoid(x):
    return 1.0 / (1.0 + jnp.exp(-x))


def _const_spec(shape):
    return pl.BlockSpec(shape, lambda *_: (0,) * len(shape), pipeline_mode=pl.Buffered(1))


def _in_proj_kernel(x_ref, gain_ref, w_ref, cos_ref, sin_ref, wsp_ref, bsp_ref, lng_ref, lnb_ref,
                    qkv0_ref, qkv1_ref, qkv2_ref, yg_ref, gate_ref, stage_ref, *, tm, d_model):
    q0, k0, v0 = 0, ATTN_WIDTH, 2 * ATTN_WIDTH
    u0 = 3 * ATTN_WIDTH
    z0 = u0 + GMLP_WIDTH
    ga0 = z0 + GMLP_WIDTH

    h = (_rms(x_ref[...]) * gain_ref[...]).astype(BF16)
    cos = cos_ref[...]
    sin = sin_ref[...]

    def proj(c0, width):
        return jnp.dot(h, w_ref[:, c0:c0 + width], preferred_element_type=F32)

    def rope(t):
        t1, t2 = t[:, :LANES], t[:, LANES:]
        return t1 * cos - t2 * sin, t2 * cos + t1 * sin

    for g, (d, out_ref) in enumerate(zip(DILATIONS, (qkv0_ref, qkv1_ref, qkv2_ref))):
        q = proj(q0 + g * GROUP_W, GROUP_W) * np.float32(HEAD_DIM ** -0.5)
        k = proj(k0 + g * GROUP_W, GROUP_W)
        v = proj(v0 + g * GROUP_W, GROUP_W)
        slabs = rope(q) + rope(k) + (v[:, :LANES], v[:, LANES:])
        if d == 1:
            for s, val in enumerate(slabs):
                out_ref[s // N_SLABS, 0, :, (s % N_SLABS) * LANES:(s % N_SLABS + 1) * LANES] = val.astype(BF16)
        else:
            for s, val in enumerate(slabs):
                stage_ref[s] = val
            for r in range(d):
                for s in range(len(slabs)):
                    rows = stage_ref[s, pl.ds(r, tm // d, stride=d), :]
                    out_ref[s // N_SLABS, r, :, (s % N_SLABS) * LANES:(s % N_SLABS + 1) * LANES] = rows.astype(BF16)

    u = _gelu_tanh(proj(u0, GMLP_WIDTH))
    z = _gelu_tanh(proj(z0, GMLP_WIDTH))
    zc = z - jnp.mean(z, axis=-1, keepdims=True)
    var = jnp.mean(zc * zc, axis=-1, keepdims=True)
    zn = (zc * lax.rsqrt(var + EPS) * lng_ref[...] + lnb_ref[...]).astype(BF16)
    n_chunks = tm // CHUNK
    pos_out = lax.broadcasted_iota(jnp.int32, (CHUNK, CHUNK), 0)
    pos_in = lax.broadcasted_iota(jnp.int32, (CHUNK, CHUNK), 1)
    causal = pos_in <= pos_out
    for g in range(GMLP_GROUPS):
        cols = slice(g * LANES, (g + 1) * LANES)
        w = jnp.where(causal, wsp_ref[g], 0.0).astype(BF16)
        zcat = jnp.concatenate([zn[c * CHUNK:(c + 1) * CHUNK, cols] for c in range(n_chunks)], axis=1)
        sz = jnp.dot(w, zcat, preferred_element_type=F32) + bsp_ref[:, g:g + 1]
        for c in range(n_chunks):
            rows = slice(c * CHUNK, (c + 1) * CHUNK)
            yg_ref[rows, cols] = (u[rows, cols] * sz[:, c * CHUNK:(c + 1) * CHUNK]).astype(BF16)

    gate_cols = 512
    for c in range(2 * d_model // gate_cols):
        gate_ref[:, c * gate_cols:(c + 1) * gate_cols] = _sigmoid(proj(ga0 + c * gate_cols, gate_cols)).astype(BF16)


def _in_proj(x, gain, w_in, cos, sin, w_sp, b_sp_t, ln_g, ln_b, *, tm):
    B, S, D = x.shape
    in_width = w_in.shape[1]
    kern = functools.partial(_in_proj_kernel, tm=tm, d_model=D)
    qkv_shapes = [jax.ShapeDtypeStruct((3, B, d, S // d, GROUP_W), BF16) for d in DILATIONS]
    qkv_specs = [pl.BlockSpec((3, None, d, tm // d, GROUP_W), lambda b, i: (0, b, 0, i, 0)) for d in DILATIONS]
    return pl.pallas_call(
        kern,
        grid=(B, S // tm),
        in_specs=[
            pl.BlockSpec((None, tm, D), lambda b, i: (b, i, 0)),
            _const_spec((1, D)),
            _const_spec((D, in_width)),
            pl.BlockSpec((tm, LANES), lambda b, i: (i, 0)),
            pl.BlockSpec((tm, LANES), lambda b, i: (i, 0)),
            _const_spec((GMLP_GROUPS, CHUNK, CHUNK)),
            _const_spec((CHUNK, GMLP_GROUPS)),
            _const_spec((1, GMLP_WIDTH)),
            _const_spec((1, GMLP_WIDTH)),
        ],
        out_specs=qkv_specs + [
            pl.BlockSpec((None, tm, GMLP_WIDTH), lambda b, i: (b, i, 0)),
            pl.BlockSpec((None, tm, 2 * D), lambda b, i: (b, i, 0)),
        ],
        out_shape=qkv_shapes + [
            jax.ShapeDtypeStruct((B, S, GMLP_WIDTH), BF16),
            jax.ShapeDtypeStruct((B, S, 2 * D), BF16),
        ],
        scratch_shapes=[pltpu.VMEM((3 * N_SLABS, tm, LANES), F32)],
        compiler_params=pltpu.CompilerParams(
            dimension_semantics=("arbitrary", "arbitrary"), vmem_limit_bytes=VMEM_LIMIT_BYTES),
        name="in_proj",
    )(x, gain, w_in, cos, sin, w_sp, b_sp_t, ln_g, ln_b)


def _attn_kernel(q_ref, k_ref, v_ref, kp_ref, vp_ref, o_ref, lse_ref, *, rb, tq):
    n_blocks = tq // QBLK
    first_tile = pl.program_id(2) == 0
    row = lax.broadcasted_iota(jnp.int32, (QBLK, 2 * QBLK), 0)
    col = lax.broadcasted_iota(jnp.int32, (QBLK, 2 * QBLK), 1)
    band = (col >= row) & (col - QBLK <= row)
    band_first = band & ((col >= QBLK) | jnp.logical_not(first_tile))
    lane = lax.broadcasted_iota(jnp.int32, (QBLK, GROUP_W), 1)
    qk_head = (lane % LANES) // HALF
    vo_head = lane // HEAD_DIM

    def attend(q, kcat, vcat, mask):
        o = jnp.zeros((QBLK, GROUP_W), F32)
        lse = jnp.zeros((QBLK, GROUP_W), F32)
        for hh in range(HEADS_PER_GROUP):
            qm = jnp.where(qk_head == hh, q, jnp.zeros_like(q))
            s = lax.dot_general(qm, kcat, (((1,), (1,)), ((), ())), preferred_element_type=F32)
            s = jnp.where(mask, s, NEG)
            m = jnp.max(s, axis=-1, keepdims=True)
            p = jnp.exp(s - m)
            den = jnp.sum(p, axis=-1, keepdims=True)
            pv = jnp.dot(p.astype(BF16), vcat, preferred_element_type=F32)
            mine = vo_head == hh
            o = jnp.where(mine, pv * (1.0 / den), o)
            lse = jnp.where(mine, m + jnp.log(den), lse)
        return o, lse

    def per_residue(rr, carry):
        kcat = jnp.concatenate([kp_ref[rr], k_ref[rr, 0:QBLK, :]], axis=0)
        vcat = jnp.concatenate([vp_ref[rr], v_ref[rr, 0:QBLK, :]], axis=0)
        o, lse = attend(q_ref[rr, 0:QBLK, :], kcat, vcat, band_first)
        o_ref[rr, 0:QBLK, :] = o.astype(BF16)
        lse_ref[rr, 0:QBLK, :] = lse

        def per_block(n, c):
            start = pl.multiple_of(n * QBLK, QBLK)
            keys = pl.ds(pl.multiple_of(start - QBLK, QBLK), 2 * QBLK)
            o, lse = attend(q_ref[rr, pl.ds(start, QBLK), :], k_ref[rr, keys, :], v_ref[rr, keys, :], band)
            o_ref[rr, pl.ds(start, QBLK), :] = o.astype(BF16)
            lse_ref[rr, pl.ds(start, QBLK), :] = lse
            return c

        return lax.fori_loop(1, n_blocks, per_block, carry)

    lax.fori_loop(0, rb, per_residue, 0)


def _attention(qkv, *, rb, tq):
    _, B, d, L, _ = qkv.shape
    bpt = tq // QBLK

    def cur(which):
        return pl.BlockSpec((None, None, rb, tq, GROUP_W), lambda b, r, i: (which, b, r, i, 0))

    def prev(which):
        return pl.BlockSpec((None, None, rb, QBLK, GROUP_W),
                            lambda b, r, i: (which, b, r, jnp.maximum(i * bpt - 1, 0), 0))

    out_spec = pl.BlockSpec((None, rb, tq, GROUP_W), lambda b, r, i: (b, r, i, 0))
    return pl.pallas_call(
        functools.partial(_attn_kernel, rb=rb, tq=tq),
        grid=(B, d // rb, L // tq),
        in_specs=[cur(0), cur(1), cur(2), prev(1), prev(2)],
        out_specs=[out_spec, out_spec],
        out_shape=[jax.ShapeDtypeStruct((B, d, L, GROUP_W), BF16),
                   jax.ShapeDtypeStruct((B, d, L, GROUP_W), F32)],
        compiler_params=pltpu.CompilerParams(
            dimension_semantics=("arbitrary", "arbitrary", "arbitrary"), vmem_limit_bytes=VMEM_LIMIT_BYTES),
        name=f"attn_d{d}",
    )(qkv, qkv, qkv, qkv, qkv)


def _out_kernel(x_ref, o0_ref, l0_ref, o1_ref, l1_ref, o2_ref, l2_ref, yg_ref, gate_ref,
                wba_ref, wbg_ref, wo_ref, g_post_mix_ref, g_pre_mlp_ref, w1_ref, w2_ref, g_post_mlp_ref,
                out_ref, stage_ref, *, tm, d_model, ff_chunk):
    def token_order(o_ref, l_ref, d, base):
        if d == 1:
            return o_ref[0].astype(F32), l_ref[0]
        for r in range(d):
            rows = pl.ds(r, tm // d, stride=d)
            for s in range(N_SLABS):
                cols = slice(s * LANES, (s + 1) * LANES)
                stage_ref[base + s, rows, :] = o_ref[r, :, cols].astype(F32)
                stage_ref[base + N_SLABS + s, rows, :] = l_ref[r, :, cols]
        o = jnp.concatenate([stage_ref[base + s] for s in range(N_SLABS)], axis=1)
        l = jnp.concatenate([stage_ref[base + N_SLABS + s] for s in range(N_SLABS)], axis=1)
        return o, l

    o0, l0 = token_order(o0_ref, l0_ref, DILATIONS[0], 0)
    o1, l1 = token_order(o1_ref, l1_ref, DILATIONS[1], 0)
    o2, l2 = token_order(o2_ref, l2_ref, DILATIONS[2], 2 * N_SLABS)
    m = jnp.maximum(jnp.maximum(l0, l1), l2)
    e0, e1, e2 = jnp.exp(l0 - m), jnp.exp(l1 - m), jnp.exp(l2 - m)
    y_attn = ((e0 * o0 + e1 * o1 + e2 * o2) / (e0 + e1 + e2)).astype(BF16)

    gate_a = gate_ref[:, :d_model].astype(F32)
    gate_b = gate_ref[:, d_model:].astype(F32)
    merged = (gate_a * jnp.dot(y_attn, wba_ref[...], preferred_element_type=F32)
              + gate_b * jnp.dot(yg_ref[...], wbg_ref[...], preferred_element_type=F32))
    y = jnp.dot(merged.astype(BF16), wo_ref[...], preferred_element_type=F32)
    x1 = x_ref[...] + _rms(y) * g_post_mix_ref[...]

    h = (_rms(x1) * g_pre_mlp_ref[...]).astype(BF16)
    acc = jnp.zeros((tm, d_model), F32)
    for c in range(w1_ref.shape[1] // ff_chunk):
        cols = slice(c * ff_chunk, (c + 1) * ff_chunk)
        a = jnp.maximum(jnp.dot(h, w1_ref[:, cols], preferred_element_type=F32), 0.0)
        acc = acc + jnp.dot((a * a).astype(BF16), w2_ref[cols, :], preferred_element_type=F32)
    out_ref[...] = x1 + _rms(acc) * g_post_mlp_ref[...]


def _out_proj(x, attn, yg, gates, wba, wbg, wo, g_post_mix, g_pre_mlp, w1, w2, g_post_mlp, *, tm, ff_chunk):
    B, S, D = x.shape
    d_ff = w1.shape[1]
    attn_args, attn_specs = [], []
    for d, (o, l) in zip(DILATIONS, attn):
        spec = pl.BlockSpec((None, d, tm // d, GROUP_W), lambda b, i: (b, 0, i, 0))
        attn_args += [o, l]
        attn_specs += [spec, spec]
    tok = lambda w: pl.BlockSpec((None, tm, w), lambda b, i: (b, i, 0))
    return pl.pallas_call(
        functools.partial(_out_kernel, tm=tm, d_model=D, ff_chunk=ff_chunk),
        grid=(B, S // tm),
        in_specs=[tok(D)] + attn_specs + [
            tok(GMLP_WIDTH), tok(2 * D),
            _const_spec((GROUP_W, D)), _const_spec((GMLP_WIDTH, D)), _const_spec((D, D)),
            _const_spec((1, D)), _const_spec((1, D)),
            _const_spec((D, d_ff)), _const_spec((d_ff, D)), _const_spec((1, D)),
        ],
        out_specs=tok(D),
        out_shape=jax.ShapeDtypeStruct((B, S, D), F32),
        scratch_shapes=[pltpu.VMEM((4 * N_SLABS, tm, LANES), F32)],
        compiler_params=pltpu.CompilerParams(
            dimension_semantics=("arbitrary", "arbitrary"), vmem_limit_bytes=VMEM_LIMIT_BYTES),
        name="out_proj_mlp",
    )(x, *attn_args, yg, gates, wba, wbg, wo, g_post_mix, g_pre_mlp, w1, w2, g_post_mlp)


def _rope_column_order():
    order = []
    for half in range(2):
        for head in range(HEADS_PER_GROUP):
            order += [head * HEAD_DIM + half * HALF + e for e in range(HALF)]
    return np.asarray(order)


def _in_proj_column_order(in_width):
    cols = np.arange(in_width)
    within = _rope_column_order()
    for base in (0, ATTN_WIDTH):
        for g in range(N_GROUPS):
            lo = base + g * GROUP_W
            cols[lo:lo + GROUP_W] = lo + within
    return cols


def _rope_tables(seq):
    inv_freq = ROPE_THETA ** (-jnp.arange(HALF, dtype=F32) / HALF)
    ang = jnp.arange(seq, dtype=F32)[:, None] * inv_freq[None, :]
    reps = LANES // HALF
    return jnp.tile(jnp.cos(ang), (1, reps)), jnp.tile(jnp.sin(ang), (1, reps))


def _attn_tiling(d, L):
    tq = min(L, 2048 // min(d, 4))
    rb = min(d, 4)
    return rb, tq


def kernel(x, norm_pre_mix, w_in, w_spatial, b_spatial, ln_v_gain, ln_v_bias, w_branch_attn, w_branch_gmlp,
           w_out, norm_post_mix, norm_pre_mlp, w_mlp_in, w_mlp_out, norm_post_mlp):
    B, S, D = x.shape
    depth = w_in.shape[0]
    tm = 512
    assert S % tm == 0 and all(S % (d * QBLK) == 0 for d in DILATIONS)
    cos, sin = _rope_tables(S)
    col_order = _in_proj_column_order(w_in.shape[2])
    row = lambda p: p.reshape(1, -1)
    for layer in range(depth):
        qkv0, qkv1, qkv2, yg, gates = _in_proj(
            x, row(norm_pre_mix[layer]), w_in[layer][:, col_order].astype(BF16), cos, sin,
            w_spatial[layer], b_spatial[layer].T, row(ln_v_gain[layer]), row(ln_v_bias[layer]), tm=tm)
        attn = []
        for qkv in (qkv0, qkv1, qkv2):
            rb, tq = _attn_tiling(qkv.shape[2], qkv.shape[3])
            attn.append(_attention(qkv, rb=rb, tq=tq))
        x = _out_proj(
            x, attn, yg, gates,
            w_branch_attn[layer].astype(BF16), w_branch_gmlp[layer].astype(BF16), w_out[layer].astype(BF16),
            row(norm_post_mix[layer]), row(norm_pre_mlp[layer]),
            w_mlp_in[layer].astype(BF16), w_mlp_out[layer].astype(BF16), row(norm_post_mlp[layer]),
            tm=tm, ff_chunk=1024)
    return x
```

```python
import functools

import numpy as np
import jax
import jax.numpy as jnp
from jax import lax
from jax.experimental import pallas as pl
from jax.experimental.pallas import tpu as pltpu

HEAD_DIM = 64
HALF = HEAD_DIM // 2
DIL_PATTERNS = ((128, 1), (512, 4), (2048, 16))
DILATIONS = tuple(d for _, d in DIL_PATTERNS)
N_GROUPS = len(DIL_PATTERNS)
HEADS_PER_GROUP = 4
GROUP_W = HEADS_PER_GROUP * HEAD_DIM
ATTN_WIDTH = N_GROUPS * GROUP_W
QBLK = 128
ROPE_THETA = 10000.0
CHUNK = 128
GMLP_GROUPS = 4
GMLP_WIDTH = GMLP_GROUPS * 128
EPS = 1e-6
NEG = -1e30
Q_SCALE = np.float32(HEAD_DIM ** -0.5 * np.log2(np.e))

LANES = 128
N_SLABS = GROUP_W // LANES

BF16 = jnp.bfloat16
F32 = jnp.float32

VMEM_LIMIT_BYTES = 56 * 1024 * 1024


def _rms(y):
    return y * lax.rsqrt(jnp.mean(y * y, axis=-1, keepdims=True) + EPS)


def _gelu_tanh(x):
    c = np.float32(np.sqrt(2.0 / np.pi))
    return x * (0.5 * (1.0 + jnp.tanh(c * (x + np.float32(0.044715) * (x * x * x)))))


def _sigmoid(x):
    return 1.0 / (1.0 + jnp.exp(-x))


def _const_spec(shape):
    return pl.BlockSpec(shape, lambda *_: (0,) * len(shape), pipeline_mode=pl.Buffered(1))


def _in_proj_kernel(x_ref, gain_ref, w_ref, cos_ref, sin_ref, wsp_ref, bsp_ref, lng_ref, lnb_ref,
                    qkv0_ref, qkv1_ref, qkv2_ref, yg_ref, gate_ref, stage_ref, *, tm, d_model):
    q0, k0, v0 = 0, ATTN_WIDTH, 2 * ATTN_WIDTH
    u0 = 3 * ATTN_WIDTH
    z0 = u0 + GMLP_WIDTH
    ga0 = z0 + GMLP_WIDTH

    h = (_rms(x_ref[...]) * gain_ref[...]).astype(BF16)
    cos = cos_ref[...]
    sin = sin_ref[...]

    def proj(c0, width):
        return jnp.dot(h, w_ref[:, c0:c0 + width], preferred_element_type=F32)

    def rope(t):
        t1, t2 = t[:, :LANES], t[:, LANES:]
        return t1 * cos - t2 * sin, t2 * cos + t1 * sin

    for g, (d, out_ref) in enumerate(zip(DILATIONS, (qkv0_ref, qkv1_ref, qkv2_ref))):
        q = proj(q0 + g * GROUP_W, GROUP_W) * Q_SCALE
        k = proj(k0 + g * GROUP_W, GROUP_W)
        v = proj(v0 + g * GROUP_W, GROUP_W)
        slabs = rope(q) + rope(k) + (v[:, :LANES], v[:, LANES:])
        if d == 1:
            for s, val in enumerate(slabs):
                out_ref[s // N_SLABS, 0, :, (s % N_SLABS) * LANES:(s % N_SLABS + 1) * LANES] = val.astype(BF16)
        else:
            for s, val in enumerate(slabs):
                stage_ref[s] = val
            for r in range(d):
                for s in range(len(slabs)):
                    rows = stage_ref[s, pl.ds(r, tm // d, stride=d), :]
                    out_ref[s // N_SLABS, r, :, (s % N_SLABS) * LANES:(s % N_SLABS + 1) * LANES] = rows.astype(BF16)

    u = _gelu_tanh(proj(u0, GMLP_WIDTH))
    z = _gelu_tanh(proj(z0, GMLP_WIDTH))
    zc = z - jnp.mean(z, axis=-1, keepdims=True)
    var = jnp.mean(zc * zc, axis=-1, keepdims=True)
    zn = (zc * lax.rsqrt(var + EPS) * lng_ref[...] + lnb_ref[...]).astype(BF16)
    n_chunks = tm // CHUNK
    pos_out = lax.broadcasted_iota(jnp.int32, (CHUNK, CHUNK), 0)
    pos_in = lax.broadcasted_iota(jnp.int32, (CHUNK, CHUNK), 1)
    causal = pos_in <= pos_out
    for g in range(GMLP_GROUPS):
        cols = slice(g * LANES, (g + 1) * LANES)
        w = jnp.where(causal, wsp_ref[g], 0.0).astype(BF16)
        zcat = jnp.concatenate([zn[c * CHUNK:(c + 1) * CHUNK, cols] for c in range(n_chunks)], axis=1)
        sz = jnp.dot(w, zcat, preferred_element_type=F32) + bsp_ref[:, g:g + 1]
        for c in range(n_chunks):
            rows = slice(c * CHUNK, (c + 1) * CHUNK)
            yg_ref[rows, cols] = (u[rows, cols] * sz[:, c * CHUNK:(c + 1) * CHUNK]).astype(BF16)

    gate_cols = 512
    for c in range(2 * d_model // gate_cols):
        gate_ref[:, c * gate_cols:(c + 1) * gate_cols] = _sigmoid(proj(ga0 + c * gate_cols, gate_cols)).astype(BF16)


def _in_proj(x, gain, w_in, cos, sin, w_sp, b_sp_t, ln_g, ln_b, *, tm):
    B, S, D = x.shape
    in_width = w_in.shape[1]
    kern = functools.partial(_in_proj_kernel, tm=tm, d_model=D)
    qkv_shapes = [jax.ShapeDtypeStruct((3, B, d, S // d, GROUP_W), BF16) for d in DILATIONS]
    qkv_specs = [pl.BlockSpec((3, None, d, tm // d, GROUP_W), lambda b, i: (0, b, 0, i, 0)) for d in DILATIONS]
    return pl.pallas_call(
        kern,
        grid=(B, S // tm),
        in_specs=[
            pl.BlockSpec((None, tm, D), lambda b, i: (b, i, 0)),
            _const_spec((1, D)),
            _const_spec((D, in_width)),
            pl.BlockSpec((tm, LANES), lambda b, i: (i, 0)),
            pl.BlockSpec((tm, LANES), lambda b, i: (i, 0)),
            _const_spec((GMLP_GROUPS, CHUNK, CHUNK)),
            _const_spec((CHUNK, GMLP_GROUPS)),
            _const_spec((1, GMLP_WIDTH)),
            _const_spec((1, GMLP_WIDTH)),
        ],
        out_specs=qkv_specs + [
            pl.BlockSpec((None, tm, GMLP_WIDTH), lambda b, i: (b, i, 0)),
            pl.BlockSpec((None, tm, 2 * D), lambda b, i: (b, i, 0)),
        ],
        out_shape=qkv_shapes + [
            jax.ShapeDtypeStruct((B, S, GMLP_WIDTH), BF16),
            jax.ShapeDtypeStruct((B, S, 2 * D), BF16),
        ],
        scratch_shapes=[pltpu.VMEM((3 * N_SLABS, tm, LANES), F32)],
        compiler_params=pltpu.CompilerParams(
            dimension_semantics=("arbitrary", "arbitrary"), vmem_limit_bytes=VMEM_LIMIT_BYTES),
        name="in_proj",
    )(x, gain, w_in, cos, sin, w_sp, b_sp_t, ln_g, ln_b)


def _attn_kernel(q_ref, k_ref, v_ref, kp_ref, vp_ref, num_ref, max_ref, den_ref, *, rb, tq):
    n_blocks = tq // QBLK
    first_tile = pl.program_id(2) == 0
    row = lax.broadcasted_iota(jnp.int32, (QBLK, 2 * QBLK), 0)
    col = lax.broadcasted_iota(jnp.int32, (QBLK, 2 * QBLK), 1)
    band = (col >= row) & (col - QBLK <= row)
    bias = jnp.where(band, 0.0, NEG).astype(F32)
    bias_first = jnp.where(band & ((col >= QBLK) | jnp.logical_not(first_tile)), 0.0, NEG).astype(F32)
    lane = lax.broadcasted_iota(jnp.int32, (QBLK, GROUP_W), 1)
    qk_head = (lane % LANES) // HALF
    heads_per_slab = LANES // HEAD_DIM
    low_head = lax.broadcasted_iota(jnp.int32, (QBLK, LANES), 1) < HEAD_DIM
    ones = jnp.ones((2 * QBLK, LANES), BF16)

    def attend(q, kcat, vcat, bias):
        qs = jnp.concatenate([jnp.where(qk_head == hh, q, jnp.zeros_like(q)) for hh in range(HEADS_PER_GROUP)],
                             axis=0)
        s = lax.dot_general(qs, kcat, (((1,), (1,)), ((), ())), preferred_element_type=F32)
        s = s.reshape(HEADS_PER_GROUP, QBLK, 2 * QBLK) + bias[None]
        m = jnp.max(s, axis=-1, keepdims=True)
        pb = jnp.exp2(s - m).astype(BF16)
        num, mx, den = [], [], []
        for t in range(N_SLABS):
            lo, hi = heads_per_slab * t, heads_per_slab * t + 1
            v_ones = jnp.concatenate([vcat[:, t * LANES:(t + 1) * LANES], ones], axis=1)
            pv = jnp.dot(pb[lo:hi + 1].reshape(heads_per_slab * QBLK, 2 * QBLK), v_ones,
                         preferred_element_type=F32)
            num.append(jnp.where(low_head, pv[:QBLK, :LANES], pv[QBLK:, :LANES]))
            den.append(jnp.where(low_head, pv[:QBLK, LANES:], pv[QBLK:, LANES:]))
            mx.append(jnp.where(low_head, m[lo], m[hi]))
        return [jnp.concatenate(parts, axis=1) for parts in (num, mx, den)]

    def store(rr, rows, num, mx, den):
        num_ref[rr, rows, :] = num.astype(BF16)
        max_ref[rr, rows, :] = mx
        den_ref[rr, rows, :] = den

    def per_residue(rr, carry):
        kcat = jnp.concatenate([kp_ref[rr], k_ref[rr, 0:QBLK, :]], axis=0)
        vcat = jnp.concatenate([vp_ref[rr], v_ref[rr, 0:QBLK, :]], axis=0)
        store(rr, slice(0, QBLK), *attend(q_ref[rr, 0:QBLK, :], kcat, vcat, bias_first))

        def per_block(n, c):
            start = pl.multiple_of(n * QBLK, QBLK)
            keys = pl.ds(pl.multiple_of(start - QBLK, QBLK), 2 * QBLK)
            store(rr, pl.ds(start, QBLK),
                  *attend(q_ref[rr, pl.ds(start, QBLK), :], k_ref[rr, keys, :], v_ref[rr, keys, :], bias))
            return c

        return lax.fori_loop(1, n_blocks, per_block, carry, unroll=4)

    lax.fori_loop(0, rb, per_residue, 0)


def _attention(qkv, *, rb, tq):
    _, B, d, L, _ = qkv.shape
    bpt = tq // QBLK

    def cur(which):
        return pl.BlockSpec((None, None, rb, tq, GROUP_W), lambda b, r, i: (which, b, r, i, 0))

    def prev(which):
        return pl.BlockSpec((None, None, rb, QBLK, GROUP_W),
                            lambda b, r, i: (which, b, r, jnp.maximum(i * bpt - 1, 0), 0))

    out_spec = pl.BlockSpec((None, rb, tq, GROUP_W), lambda b, r, i: (b, r, i, 0))
    return pl.pallas_call(
        functools.partial(_attn_kernel, rb=rb, tq=tq),
        grid=(B, d // rb, L // tq),
        in_specs=[cur(0), cur(1), cur(2), prev(1), prev(2)],
        out_specs=[out_spec, out_spec, out_spec],
        out_shape=[jax.ShapeDtypeStruct((B, d, L, GROUP_W), BF16),
                   jax.ShapeDtypeStruct((B, d, L, GROUP_W), F32),
                   jax.ShapeDtypeStruct((B, d, L, GROUP_W), F32)],
        compiler_params=pltpu.CompilerParams(
            dimension_semantics=("arbitrary", "arbitrary", "arbitrary"), vmem_limit_bytes=VMEM_LIMIT_BYTES),
        name=f"attn_d{d}",
    )(qkv, qkv, qkv, qkv, qkv)


def _out_kernel(x_ref, num0_ref, max0_ref, den0_ref, num1_ref, max1_ref, den1_ref, num2_ref, max2_ref, den2_ref,
                yg_ref, gate_ref, wba_ref, wbg_ref, wo_ref, g_post_mix_ref, g_pre_mlp_ref, w1_ref, w2_ref,
                g_post_mlp_ref, out_ref, stage_ref, *, tm, d_model, ff_chunk):
    def token_order(refs, d, base):
        if d == 1:
            return [ref[0].astype(F32) for ref in refs]
        for r in range(d):
            rows = pl.ds(r, tm // d, stride=d)
            for a, ref in enumerate(refs):
                for s in range(N_SLABS):
                    stage_ref[base + a * N_SLABS + s, rows, :] = ref[r, :, s * LANES:(s + 1) * LANES].astype(F32)
        return [jnp.concatenate([stage_ref[base + a * N_SLABS + s] for s in range(N_SLABS)], axis=1)
                for a in range(len(refs))]

    n0, m0, d0 = token_order((num0_ref, max0_ref, den0_ref), DILATIONS[0], 0)
    n1, m1, d1 = token_order((num1_ref, max1_ref, den1_ref), DILATIONS[1], 0)
    n2, m2, d2 = token_order((num2_ref, max2_ref, den2_ref), DILATIONS[2], 3 * N_SLABS)
    m = jnp.maximum(jnp.maximum(m0, m1), m2)
    e0, e1, e2 = jnp.exp2(m0 - m), jnp.exp2(m1 - m), jnp.exp2(m2 - m)
    y_attn = ((e0 * n0 + e1 * n1 + e2 * n2) / (e0 * d0 + e1 * d1 + e2 * d2)).astype(BF16)

    gate_a = gate_ref[:, :d_model].astype(F32)
    gate_b = gate_ref[:, d_model:].astype(F32)
    merged = (gate_a * jnp.dot(y_attn, wba_ref[...], preferred_element_type=F32)
              + gate_b * jnp.dot(yg_ref[...], wbg_ref[...], preferred_element_type=F32))
    y = jnp.dot(merged.astype(BF16), wo_ref[...], preferred_element_type=F32)
    x1 = x_ref[...] + _rms(y) * g_post_mix_ref[...]

    h = (_rms(x1) * g_pre_mlp_ref[...]).astype(BF16)
    acc = jnp.zeros((tm, d_model), F32)
    for c in range(w1_ref.shape[1] // ff_chunk):
        cols = slice(c * ff_chunk, (c + 1) * ff_chunk)
        a = jnp.maximum(jnp.dot(h, w1_ref[:, cols], preferred_element_type=F32), 0.0)
        acc = acc + jnp.dot((a * a).astype(BF16), w2_ref[cols, :], preferred_element_type=F32)
    out_ref[...] = x1 + _rms(acc) * g_post_mlp_ref[...]


def _out_proj(x, attn, yg, gates, wba, wbg, wo, g_post_mix, g_pre_mlp, w1, w2, g_post_mlp, *, tm, ff_chunk):
    B, S, D = x.shape
    d_ff = w1.shape[1]
    attn_args, attn_specs = [], []
    for d, parts in zip(DILATIONS, attn):
        spec = pl.BlockSpec((None, d, tm // d, GROUP_W), lambda b, i: (b, 0, i, 0))
        attn_args += list(parts)
        attn_specs += [spec] * len(parts)
    tok = lambda w: pl.BlockSpec((None, tm, w), lambda b, i: (b, i, 0))
    return pl.pallas_call(
        functools.partial(_out_kernel, tm=tm, d_model=D, ff_chunk=ff_chunk),
        grid=(B, S // tm),
        in_specs=[tok(D)] + attn_specs + [
            tok(GMLP_WIDTH), tok(2 * D),
            _const_spec((GROUP_W, D)), _const_spec((GMLP_WIDTH, D)), _const_spec((D, D)),
            _const_spec((1, D)), _const_spec((1, D)),
            _const_spec((D, d_ff)), _const_spec((d_ff, D)), _const_spec((1, D)),
        ],
        out_specs=tok(D),
        out_shape=jax.ShapeDtypeStruct((B, S, D), F32),
        scratch_shapes=[pltpu.VMEM((2 * 3 * N_SLABS, tm, LANES), F32)],
        compiler_params=pltpu.CompilerParams(
            dimension_semantics=("arbitrary", "arbitrary"), vmem_limit_bytes=VMEM_LIMIT_BYTES),
        name="out_proj_mlp",
    )(x, *attn_args, yg, gates, wba, wbg, wo, g_post_mix, g_pre_mlp, w1, w2, g_post_mlp)


def _rope_column_order():
    order = []
    for half in range(2):
        for head in range(HEADS_PER_GROUP):
            order += [head * HEAD_DIM + half * HALF + e for e in range(HALF)]
    return np.asarray(order)


def _in_proj_column_order(in_width):
    cols = np.arange(in_width)
    within = _rope_column_order()
    for base in (0, ATTN_WIDTH):
        for g in range(N_GROUPS):
            lo = base + g * GROUP_W
            cols[lo:lo + GROUP_W] = lo + within
    return cols


def _rope_tables(seq):
    inv_freq = ROPE_THETA ** (-jnp.arange(HALF, dtype=F32) / HALF)
    ang = jnp.arange(seq, dtype=F32)[:, None] * inv_freq[None, :]
    reps = LANES // HALF
    return jnp.tile(jnp.cos(ang), (1, reps)), jnp.tile(jnp.sin(ang), (1, reps))


def _attn_tiling(d, L):
    tq = min(L, 2048 // min(d, 4))
    rb = min(d, 4)
    return rb, tq


def kernel(x, norm_pre_mix, w_in, w_spatial, b_spatial, ln_v_gain, ln_v_bias, w_branch_attn, w_branch_gmlp,
           w_out, norm_post_mix, norm_pre_mlp, w_mlp_in, w_mlp_out, norm_post_mlp):
    B, S, D = x.shape
    depth = w_in.shape[0]
    tm = 512
    assert S % tm == 0 and all(S % (d * QBLK) == 0 for d in DILATIONS)
    cos, sin = _rope_tables(S)
    col_order = _in_proj_column_order(w_in.shape[2])
    row = lambda p: p.reshape(1, -1)
    for layer in range(depth):
        qkv0, qkv1, qkv2, yg, gates = _in_proj(
            x, row(norm_pre_mix[layer]), w_in[layer][:, col_order].astype(BF16), cos, sin,
            w_spatial[layer], b_spatial[layer].T, row(ln_v_gain[layer]), row(ln_v_bias[layer]), tm=tm)
        attn = []
        for qkv in (qkv0, qkv1, qkv2):
            rb, tq = _attn_tiling(qkv.shape[2], qkv.shape[3])
            attn.append(_attention(qkv, rb=rb, tq=tq))
        x = _out_proj(
            x, attn, yg, gates,
            w_branch_attn[layer].astype(BF16), w_branch_gmlp[layer].astype(BF16), w_out[layer].astype(BF16),
            row(norm_post_mix[layer]), row(norm_pre_mlp[layer]),
            w_mlp_in[layer].astype(BF16), w_mlp_out[layer].astype(BF16), row(norm_post_mlp[layer]),
            tm=tm, ff_chunk=1024)
    return x
```

```python
import functools

import numpy as np
import jax
import jax.numpy as jnp
from jax import lax
from jax.experimental import pallas as pl
from jax.experimental.pallas import tpu as pltpu

HEAD_DIM = 64
HALF = HEAD_DIM // 2
DIL_PATTERNS = ((128, 1), (512, 4), (2048, 16))
DILATIONS = tuple(d for _, d in DIL_PATTERNS)
N_GROUPS = len(DIL_PATTERNS)
HEADS_PER_GROUP = 4
GROUP_W = HEADS_PER_GROUP * HEAD_DIM
ATTN_WIDTH = N_GROUPS * GROUP_W
QBLK = 128
ROPE_THETA = 10000.0
CHUNK = 128
GMLP_GROUPS = 4
GMLP_WIDTH = GMLP_GROUPS * 128
EPS = 1e-6
NEG = -1e30
Q_SCALE = np.float32(HEAD_DIM ** -0.5 * np.log2(np.e))

LANES = 128
N_SLABS = GROUP_W // LANES

BF16 = jnp.bfloat16
F32 = jnp.float32

VMEM_LIMIT_BYTES = 56 * 1024 * 1024
TM_IN_PROJ = 1024
TM_OUT_PROJ = 512
FF_CHUNK = 1024


def _rms(y):
    return y * lax.rsqrt(jnp.mean(y * y, axis=-1, keepdims=True) + EPS)


def _gelu_tanh(x):
    c = np.float32(np.sqrt(2.0 / np.pi))
    return x * (0.5 * (1.0 + jnp.tanh(c * (x + np.float32(0.044715) * (x * x * x)))))


def _sigmoid(x):
    return 1.0 / (1.0 + jnp.exp(-x))


def _const_spec(shape):
    return pl.BlockSpec(shape, lambda *_: (0,) * len(shape), pipeline_mode=pl.Buffered(1))


def _in_proj_kernel(x_ref, gain_ref, w_ref, cos_ref, sin_ref, wsp_ref, bsp_ref, lng_ref, lnb_ref,
                    qkv_ref, yg_ref, gate_ref, *, tm, d_model):
    q0, k0, v0 = 0, ATTN_WIDTH, 2 * ATTN_WIDTH
    u0 = 3 * ATTN_WIDTH
    z0 = u0 + GMLP_WIDTH
    ga0 = z0 + GMLP_WIDTH

    h = (_rms(x_ref[...]) * gain_ref[...]).astype(BF16)
    cos = cos_ref[...]
    sin = sin_ref[...]

    def proj(c0, width):
        return jnp.dot(h, w_ref[:, c0:c0 + width], preferred_element_type=F32)

    def rope(t):
        t1, t2 = t[:, :LANES], t[:, LANES:]
        return jnp.concatenate([t1 * cos - t2 * sin, t2 * cos + t1 * sin], axis=1)

    for g in range(N_GROUPS):
        qkv_ref[3 * g + 0] = rope(proj(q0 + g * GROUP_W, GROUP_W) * Q_SCALE).astype(BF16)
        qkv_ref[3 * g + 1] = rope(proj(k0 + g * GROUP_W, GROUP_W)).astype(BF16)
        qkv_ref[3 * g + 2] = proj(v0 + g * GROUP_W, GROUP_W).astype(BF16)

    u = _gelu_tanh(proj(u0, GMLP_WIDTH))
    z = _gelu_tanh(proj(z0, GMLP_WIDTH))
    zc = z - jnp.mean(z, axis=-1, keepdims=True)
    var = jnp.mean(zc * zc, axis=-1, keepdims=True)
    zn = (zc * lax.rsqrt(var + EPS) * lng_ref[...] + lnb_ref[...]).astype(BF16)
    n_chunks = tm // CHUNK
    pos_out = lax.broadcasted_iota(jnp.int32, (CHUNK, CHUNK), 0)
    pos_in = lax.broadcasted_iota(jnp.int32, (CHUNK, CHUNK), 1)
    causal = pos_in <= pos_out
    for g in range(GMLP_GROUPS):
        cols = slice(g * LANES, (g + 1) * LANES)
        w = jnp.where(causal, wsp_ref[g], 0.0).astype(BF16)
        zcat = jnp.concatenate([zn[c * CHUNK:(c + 1) * CHUNK, cols] for c in range(n_chunks)], axis=1)
        sz = jnp.dot(w, zcat, preferred_element_type=F32) + bsp_ref[:, g:g + 1]
        for c in range(n_chunks):
            rows = slice(c * CHUNK, (c + 1) * CHUNK)
            yg_ref[rows, cols] = (u[rows, cols] * sz[:, c * CHUNK:(c + 1) * CHUNK]).astype(BF16)

    gate_cols = 512
    for c in range(2 * d_model // gate_cols):
        gate_ref[:, c * gate_cols:(c + 1) * gate_cols] = _sigmoid(proj(ga0 + c * gate_cols, gate_cols)).astype(BF16)


def _in_proj(x, gain, w_in, cos, sin, w_sp, b_sp_t, ln_g, ln_b, *, tm):
    B, S, D = x.shape
    in_width = w_in.shape[1]
    kern = functools.partial(_in_proj_kernel, tm=tm, d_model=D)
    n_qkv = 3 * N_GROUPS
    return pl.pallas_call(
        kern,
        grid=(B, S // tm),
        in_specs=[
            pl.BlockSpec((None, tm, D), lambda b, i: (b, i, 0)),
            _const_spec((1, D)),
            _const_spec((D, in_width)),
            pl.BlockSpec((tm, LANES), lambda b, i: (i, 0)),
            pl.BlockSpec((tm, LANES), lambda b, i: (i, 0)),
            _const_spec((GMLP_GROUPS, CHUNK, CHUNK)),
            _const_spec((CHUNK, GMLP_GROUPS)),
            _const_spec((1, GMLP_WIDTH)),
            _const_spec((1, GMLP_WIDTH)),
        ],
        out_specs=[
            pl.BlockSpec((n_qkv, None, tm, GROUP_W), lambda b, i: (0, b, i, 0)),
            pl.BlockSpec((None, tm, GMLP_WIDTH), lambda b, i: (b, i, 0)),
            pl.BlockSpec((None, tm, 2 * D), lambda b, i: (b, i, 0)),
        ],
        out_shape=[
            jax.ShapeDtypeStruct((n_qkv, B, S, GROUP_W), BF16),
            jax.ShapeDtypeStruct((B, S, GMLP_WIDTH), BF16),
            jax.ShapeDtypeStruct((B, S, 2 * D), BF16),
        ],
        compiler_params=pltpu.CompilerParams(
            dimension_semantics=("arbitrary", "arbitrary"), vmem_limit_bytes=VMEM_LIMIT_BYTES),
        name="in_proj",
    )(x, gain, w_in, cos, sin, w_sp, b_sp_t, ln_g, ln_b)


def _attn_kernel(q_ref, k_ref, v_ref, kp_ref, vp_ref, num_ref, max_ref, den_ref, *, rb, tq):
    n_blocks = tq // QBLK
    first_tile = pl.program_id(2) == 0
    row = lax.broadcasted_iota(jnp.int32, (QBLK, 2 * QBLK), 0)
    col = lax.broadcasted_iota(jnp.int32, (QBLK, 2 * QBLK), 1)
    band = (col >= row) & (col - QBLK <= row)
    bias = jnp.where(band, 0.0, NEG).astype(F32)
    bias_first = jnp.where(band & ((col >= QBLK) | jnp.logical_not(first_tile)), 0.0, NEG).astype(F32)
    lane = lax.broadcasted_iota(jnp.int32, (QBLK, GROUP_W), 1)
    qk_head = (lane % LANES) // HALF
    heads_per_slab = LANES // HEAD_DIM
    low_head = lax.broadcasted_iota(jnp.int32, (QBLK, LANES), 1) < HEAD_DIM
    ones = jnp.ones((2 * QBLK, LANES), BF16)

    def attend(q, kcat, vcat, bias):
        qs = jnp.concatenate([jnp.where(qk_head == hh, q, jnp.zeros_like(q)) for hh in range(HEADS_PER_GROUP)],
                             axis=0)
        s = lax.dot_general(qs, kcat, (((1,), (1,)), ((), ())), preferred_element_type=F32)
        s = s.reshape(HEADS_PER_GROUP, QBLK, 2 * QBLK) + bias[None]
        m = jnp.max(s, axis=-1, keepdims=True)
        pb = jnp.exp2(s - m).astype(BF16)
        num, mx, den = [], [], []
        for t in range(N_SLABS):
            lo, hi = heads_per_slab * t, heads_per_slab * t + 1
            v_ones = jnp.concatenate([vcat[:, t * LANES:(t + 1) * LANES], ones], axis=1)
            pv = jnp.dot(pb[lo:hi + 1].reshape(heads_per_slab * QBLK, 2 * QBLK), v_ones,
                         preferred_element_type=F32)
            num.append(jnp.where(low_head, pv[:QBLK, :LANES], pv[QBLK:, :LANES]))
            den.append(jnp.where(low_head, pv[:QBLK, LANES:], pv[QBLK:, LANES:]))
            mx.append(jnp.where(low_head, m[lo], m[hi]))
        return [jnp.concatenate(parts, axis=1) for parts in (num, mx, den)]

    def store(rows, cols, num, mx, den):
        num_ref[rows, cols] = num.astype(BF16)
        max_ref[rows, cols] = mx
        den_ref[rows, cols] = den

    for rr in range(rb):
        cols = slice(rr * GROUP_W, (rr + 1) * GROUP_W)
        kcat = jnp.concatenate([kp_ref[:, cols], k_ref[0:QBLK, cols]], axis=0)
        vcat = jnp.concatenate([vp_ref[:, cols], v_ref[0:QBLK, cols]], axis=0)
        store(slice(0, QBLK), cols, *attend(q_ref[0:QBLK, cols], kcat, vcat, bias_first))
        for n in range(1, n_blocks):
            rows = slice(n * QBLK, (n + 1) * QBLK)
            keys = slice((n - 1) * QBLK, (n + 1) * QBLK)
            store(rows, cols, *attend(q_ref[rows, cols], k_ref[keys, cols], v_ref[keys, cols], bias))


def _attention(qkv, g, *, rb, tq):
    n_qkv, B, S, _ = qkv.shape
    d = DILATIONS[g]
    L = S // d
    bpt = tq // QBLK
    qkv = qkv.reshape(n_qkv, B, L, d * GROUP_W)

    def cur(which):
        return pl.BlockSpec((None, None, tq, rb * GROUP_W), lambda b, r, i: (3 * g + which, b, i, r))

    def prev(which):
        return pl.BlockSpec((None, None, QBLK, rb * GROUP_W),
                            lambda b, r, i: (3 * g + which, b, jnp.maximum(i * bpt - 1, 0), r))

    out_spec = pl.BlockSpec((None, tq, rb * GROUP_W), lambda b, r, i: (b, i, r))
    outs = pl.pallas_call(
        functools.partial(_attn_kernel, rb=rb, tq=tq),
        grid=(B, d // rb, L // tq),
        in_specs=[cur(0), cur(1), cur(2), prev(1), prev(2)],
        out_specs=[out_spec, out_spec, out_spec],
        out_shape=[jax.ShapeDtypeStruct((B, L, d * GROUP_W), BF16),
                   jax.ShapeDtypeStruct((B, L, d * GROUP_W), F32),
                   jax.ShapeDtypeStruct((B, L, d * GROUP_W), F32)],
        compiler_params=pltpu.CompilerParams(
            dimension_semantics=("arbitrary", "arbitrary", "arbitrary"), vmem_limit_bytes=VMEM_LIMIT_BYTES),
        name=f"attn_d{d}",
    )(qkv, qkv, qkv, qkv, qkv)
    return [o.reshape(B, S, GROUP_W) for o in outs]


def _out_kernel(x_ref, num0_ref, max0_ref, den0_ref, num1_ref, max1_ref, den1_ref, num2_ref, max2_ref, den2_ref,
                yg_ref, gate_ref, wba_ref, wbg_ref, wo_ref, g_post_mix_ref, g_pre_mlp_ref, w1_ref, w2_ref,
                g_post_mlp_ref, out_ref, *, tm, d_model, ff_chunk):
    n0, m0, d0 = num0_ref[...].astype(F32), max0_ref[...], den0_ref[...]
    n1, m1, d1 = num1_ref[...].astype(F32), max1_ref[...], den1_ref[...]
    n2, m2, d2 = num2_ref[...].astype(F32), max2_ref[...], den2_ref[...]
    m = jnp.maximum(jnp.maximum(m0, m1), m2)
    e0, e1, e2 = jnp.exp2(m0 - m), jnp.exp2(m1 - m), jnp.exp2(m2 - m)
    y_attn = ((e0 * n0 + e1 * n1 + e2 * n2) / (e0 * d0 + e1 * d1 + e2 * d2)).astype(BF16)

    gate_a = gate_ref[:, :d_model].astype(F32)
    gate_b = gate_ref[:, d_model:].astype(F32)
    merged = (gate_a * jnp.dot(y_attn, wba_ref[...], preferred_element_type=F32)
              + gate_b * jnp.dot(yg_ref[...], wbg_ref[...], preferred_element_type=F32))
    y = jnp.dot(merged.astype(BF16), wo_ref[...], preferred_element_type=F32)
    x1 = x_ref[...] + _rms(y) * g_post_mix_ref[...]

    h = (_rms(x1) * g_pre_mlp_ref[...]).astype(BF16)
    acc = jnp.zeros((tm, d_model), F32)
    for c in range(w1_ref.shape[1] // ff_chunk):
        cols = slice(c * ff_chunk, (c + 1) * ff_chunk)
        a = jnp.maximum(jnp.dot(h, w1_ref[:, cols], preferred_element_type=F32), 0.0)
        acc = acc + jnp.dot((a * a).astype(BF16), w2_ref[cols, :], preferred_element_type=F32)
    out_ref[...] = x1 + _rms(acc) * g_post_mlp_ref[...]


def _out_proj(x, attn, yg, gates, wba, wbg, wo, g_post_mix, g_pre_mlp, w1, w2, g_post_mlp, *, tm, ff_chunk):
    B, S, D = x.shape
    d_ff = w1.shape[1]
    tok = lambda w: pl.BlockSpec((None, tm, w), lambda b, i: (b, i, 0))
    attn_args = [part for parts in attn for part in parts]
    attn_specs = [tok(GROUP_W)] * len(attn_args)
    return pl.pallas_call(
        functools.partial(_out_kernel, tm=tm, d_model=D, ff_chunk=ff_chunk),
        grid=(B, S // tm),
        in_specs=[tok(D)] + attn_specs + [
            tok(GMLP_WIDTH), tok(2 * D),
            _const_spec((GROUP_W, D)), _const_spec((GMLP_WIDTH, D)), _const_spec((D, D)),
            _const_spec((1, D)), _const_spec((1, D)),
            _const_spec((D, d_ff)), _const_spec((d_ff, D)), _const_spec((1, D)),
        ],
        out_specs=tok(D),
        out_shape=jax.ShapeDtypeStruct((B, S, D), F32),
        compiler_params=pltpu.CompilerParams(
            dimension_semantics=("arbitrary", "arbitrary"), vmem_limit_bytes=VMEM_LIMIT_BYTES),
        name="out_proj_mlp",
    )(x, *attn_args, yg, gates, wba, wbg, wo, g_post_mix, g_pre_mlp, w1, w2, g_post_mlp)


def _in_proj_weight(w_in):
    d_model = w_in.shape[0]
    qk = w_in[:, :2 * ATTN_WIDTH].reshape(d_model, 2 * N_GROUPS, HEADS_PER_GROUP, 2, HALF)
    qk = qk.transpose(0, 1, 3, 2, 4).reshape(d_model, 2 * ATTN_WIDTH)
    return jnp.concatenate([qk, w_in[:, 2 * ATTN_WIDTH:]], axis=1).astype(BF16)


def _rope_tables(seq):
    inv_freq = ROPE_THETA ** (-jnp.arange(HALF, dtype=F32) / HALF)
    ang = jnp.arange(seq, dtype=F32)[:, None] * inv_freq[None, :]
    reps = LANES // HALF
    return jnp.tile(jnp.cos(ang), (1, reps)), jnp.tile(jnp.sin(ang), (1, reps))


def _attn_tiling(d, L):
    tq = min(L, 2048 // min(d, 4))
    rb = min(d, 4)
    return rb, tq


def kernel(x, norm_pre_mix, w_in, w_spatial, b_spatial, ln_v_gain, ln_v_bias, w_branch_attn, w_branch_gmlp,
           w_out, norm_post_mix, norm_pre_mlp, w_mlp_in, w_mlp_out, norm_post_mlp):
    B, S, D = x.shape
    depth = w_in.shape[0]
    tm_in, tm_out = TM_IN_PROJ, TM_OUT_PROJ
    assert S % tm_in == 0 and S % tm_out == 0 and all(S % (d * QBLK) == 0 for d in DILATIONS)
    cos, sin = _rope_tables(S)
    row = lambda p: p.reshape(1, -1)
    for layer in range(depth):
        qkv, yg, gates = _in_proj(
            x, row(norm_pre_mix[layer]), _in_proj_weight(w_in[layer]), cos, sin,
            w_spatial[layer], b_spatial[layer].T, row(ln_v_gain[layer]), row(ln_v_bias[layer]), tm=tm_in)
        attn = []
        for g, d in enumerate(DILATIONS):
            rb, tq = _attn_tiling(d, S // d)
            attn.append(_attention(qkv, g, rb=rb, tq=tq))
        x = _out_proj(
            x, attn, yg, gates,
            w_branch_attn[layer].astype(BF16), w_branch_gmlp[layer].astype(BF16), w_out[layer].astype(BF16),
            row(norm_post_mix[layer]), row(norm_pre_mlp[layer]),
            w_mlp_in[layer].astype(BF16), w_mlp_out[layer].astype(BF16), row(norm_post_mlp[layer]),
            tm=tm_out, ff_chunk=FF_CHUNK)
    return x
```

```python
import functools

import numpy as np
import jax
import jax.numpy as jnp
from jax import lax
from jax.experimental import pallas as pl
from jax.experimental.pallas import tpu as pltpu

HEAD_DIM = 64
HALF = HEAD_DIM // 2
DIL_PATTERNS = ((128, 1), (512, 4), (2048, 16))
DILATIONS = tuple(d for _, d in DIL_PATTERNS)
N_GROUPS = len(DIL_PATTERNS)
HEADS_PER_GROUP = 4
GROUP_W = HEADS_PER_GROUP * HEAD_DIM
ATTN_WIDTH = N_GROUPS * GROUP_W
QBLK = 128
ROPE_THETA = 10000.0
CHUNK = 128
GMLP_GROUPS = 4
GMLP_WIDTH = GMLP_GROUPS * 128
EPS = 1e-6
NEG = -1e30
Q_SCALE = np.float32(HEAD_DIM ** -0.5 * np.log2(np.e))

LANES = 128
N_SLABS = GROUP_W // LANES

BF16 = jnp.bfloat16
F32 = jnp.float32

VMEM_LIMIT_BYTES = 56 * 1024 * 1024
TM_IN_PROJ = 1024
TM_OUT_PROJ = 512
FF_CHUNK = 1024


def _rms(y):
    return y * lax.rsqrt(jnp.mean(y * y, axis=-1, keepdims=True) + EPS)


def _gelu_tanh(x):
    c = np.float32(np.sqrt(2.0 / np.pi))
    return x * (0.5 * (1.0 + jnp.tanh(c * (x + np.float32(0.044715) * (x * x * x)))))


def _sigmoid(x):
    return 1.0 / (1.0 + jnp.exp(-x))


def _const_spec(shape):
    return pl.BlockSpec(shape, lambda *_: (0,) * len(shape), pipeline_mode=pl.Buffered(1))


def _in_proj_kernel(x_ref, gain_ref, w_ref, cos_ref, sin_ref, wsp_ref, bsp_ref, lng_ref, lnb_ref,
                    qkv0_ref, qkv1_ref, qkv2_ref, yg_ref, gate_ref, stage_ref, *, tm, d_model):
    q0, k0, v0 = 0, ATTN_WIDTH, 2 * ATTN_WIDTH
    u0 = 3 * ATTN_WIDTH
    z0 = u0 + GMLP_WIDTH
    ga0 = z0 + GMLP_WIDTH

    h = (_rms(x_ref[...]) * gain_ref[...]).astype(BF16)
    cos = cos_ref[...]
    sin = sin_ref[...]

    def proj(c0, width):
        return jnp.dot(h, w_ref[:, c0:c0 + width], preferred_element_type=F32)

    def rope(t):
        t1, t2 = t[:, :LANES], t[:, LANES:]
        return t1 * cos - t2 * sin, t2 * cos + t1 * sin

    for g, (d, out_ref) in enumerate(zip(DILATIONS, (qkv0_ref, qkv1_ref, qkv2_ref))):
        q = proj(q0 + g * GROUP_W, GROUP_W) * Q_SCALE
        k = proj(k0 + g * GROUP_W, GROUP_W)
        v = proj(v0 + g * GROUP_W, GROUP_W)
        slabs = rope(q) + rope(k) + (v[:, :LANES], v[:, LANES:])
        for s, val in enumerate(slabs):
            which, cols = s // N_SLABS, slice((s % N_SLABS) * LANES, (s % N_SLABS + 1) * LANES)
            if d == 1:
                out_ref[which, 0, :, cols] = val.astype(BF16)
            else:
                stage_ref[s] = val
                for r in range(d):
                    out_ref[which, r, :, cols] = stage_ref[s, pl.ds(r, tm // d, stride=d), :].astype(BF16)

    u = _gelu_tanh(proj(u0, GMLP_WIDTH))
    z = _gelu_tanh(proj(z0, GMLP_WIDTH))
    zc = z - jnp.mean(z, axis=-1, keepdims=True)
    var = jnp.mean(zc * zc, axis=-1, keepdims=True)
    zn = (zc * lax.rsqrt(var + EPS) * lng_ref[...] + lnb_ref[...]).astype(BF16)
    n_chunks = tm // CHUNK
    pos_out = lax.broadcasted_iota(jnp.int32, (CHUNK, CHUNK), 0)
    pos_in = lax.broadcasted_iota(jnp.int32, (CHUNK, CHUNK), 1)
    causal = pos_in <= pos_out
    for g in range(GMLP_GROUPS):
        cols = slice(g * LANES, (g + 1) * LANES)
        w = jnp.where(causal, wsp_ref[g], 0.0).astype(BF16)
        zcat = jnp.concatenate([zn[c * CHUNK:(c + 1) * CHUNK, cols] for c in range(n_chunks)], axis=1)
        sz = jnp.dot(w, zcat, preferred_element_type=F32) + bsp_ref[:, g:g + 1]
        for c in range(n_chunks):
            rows = slice(c * CHUNK, (c + 1) * CHUNK)
            yg_ref[rows, cols] = (u[rows, cols] * sz[:, c * CHUNK:(c + 1) * CHUNK]).astype(BF16)

    gate_cols = 512
    for c in range(2 * d_model // gate_cols):
        gate_ref[:, c * gate_cols:(c + 1) * gate_cols] = _sigmoid(proj(ga0 + c * gate_cols, gate_cols)).astype(BF16)


def _in_proj(x, gain, w_in, cos, sin, w_sp, b_sp_t, ln_g, ln_b, *, tm):
    B, S, D = x.shape
    in_width = w_in.shape[1]
    kern = functools.partial(_in_proj_kernel, tm=tm, d_model=D)
    qkv_shapes = [jax.ShapeDtypeStruct((3, B, d, S // d, GROUP_W), BF16) for d in DILATIONS]
    qkv_specs = [pl.BlockSpec((3, None, d, tm // d, GROUP_W), lambda b, i: (0, b, 0, i, 0)) for d in DILATIONS]
    return pl.pallas_call(
        kern,
        grid=(B, S // tm),
        in_specs=[
            pl.BlockSpec((None, tm, D), lambda b, i: (b, i, 0)),
            _const_spec((1, D)),
            _const_spec((D, in_width)),
            pl.BlockSpec((tm, LANES), lambda b, i: (i, 0)),
            pl.BlockSpec((tm, LANES), lambda b, i: (i, 0)),
            _const_spec((GMLP_GROUPS, CHUNK, CHUNK)),
            _const_spec((CHUNK, GMLP_GROUPS)),
            _const_spec((1, GMLP_WIDTH)),
            _const_spec((1, GMLP_WIDTH)),
        ],
        out_specs=qkv_specs + [
            pl.BlockSpec((None, tm, GMLP_WIDTH), lambda b, i: (b, i, 0)),
            pl.BlockSpec((None, tm, 2 * D), lambda b, i: (b, i, 0)),
        ],
        out_shape=qkv_shapes + [
            jax.ShapeDtypeStruct((B, S, GMLP_WIDTH), BF16),
            jax.ShapeDtypeStruct((B, S, 2 * D), BF16),
        ],
        scratch_shapes=[pltpu.VMEM((3 * N_SLABS, tm, LANES), F32)],
        compiler_params=pltpu.CompilerParams(
            dimension_semantics=("arbitrary", "arbitrary"), vmem_limit_bytes=VMEM_LIMIT_BYTES),
        name="in_proj",
    )(x, gain, w_in, cos, sin, w_sp, b_sp_t, ln_g, ln_b)


def _attn_kernel(q_ref, k_ref, v_ref, kp_ref, vp_ref, o_ref, lse_ref, *, rb, tq):
    n_blocks = tq // QBLK
    first_tile = pl.program_id(2) == 0
    row = lax.broadcasted_iota(jnp.int32, (QBLK, 2 * QBLK), 0)
    col = lax.broadcasted_iota(jnp.int32, (QBLK, 2 * QBLK), 1)
    band = (col >= row) & (col - QBLK <= row)
    bias = jnp.where(band, 0.0, NEG).astype(F32)
    bias_first = jnp.where(band & ((col >= QBLK) | jnp.logical_not(first_tile)), 0.0, NEG).astype(F32)
    lane = lax.broadcasted_iota(jnp.int32, (QBLK, GROUP_W), 1)
    qk_head = (lane % LANES) // HALF
    heads_per_slab = LANES // HEAD_DIM
    low_head = lax.broadcasted_iota(jnp.int32, (QBLK, LANES), 1) < HEAD_DIM
    ones = jnp.ones((2 * QBLK, LANES), BF16)

    def attend(q, kcat, vcat, bias):
        qs = jnp.concatenate([jnp.where(qk_head == hh, q, jnp.zeros_like(q)) for hh in range(HEADS_PER_GROUP)],
                             axis=0)
        s = lax.dot_general(qs, kcat, (((1,), (1,)), ((), ())), preferred_element_type=F32)
        s = s.reshape(HEADS_PER_GROUP, QBLK, 2 * QBLK) + bias[None]
        m = jnp.max(s, axis=-1, keepdims=True)
        pb = jnp.exp2(s - m).astype(BF16)
        o, lse = [], []
        for t in range(N_SLABS):
            lo, hi = heads_per_slab * t, heads_per_slab * t + 1
            v_ones = jnp.concatenate([vcat[:, t * LANES:(t + 1) * LANES], ones], axis=1)
            pv = jnp.dot(pb[lo:hi + 1].reshape(heads_per_slab * QBLK, 2 * QBLK), v_ones,
                         preferred_element_type=F32)
            num = jnp.where(low_head, pv[:QBLK, :LANES], pv[QBLK:, :LANES])
            den = jnp.where(low_head, pv[:QBLK, LANES:], pv[QBLK:, LANES:])
            o.append(num * (1.0 / den))
            lse.append(jnp.where(low_head, m[lo], m[hi]) + jnp.log2(den))
        return jnp.concatenate(o, axis=1), jnp.concatenate(lse, axis=1)

    def store(rr, rows, o, lse):
        o_ref[rr, rows, :] = o.astype(BF16)
        lse_ref[rr, rows, :] = lse

    for rr in range(rb):
        kcat = jnp.concatenate([kp_ref[rr], k_ref[rr, 0:QBLK, :]], axis=0)
        vcat = jnp.concatenate([vp_ref[rr], v_ref[rr, 0:QBLK, :]], axis=0)
        store(rr, slice(0, QBLK), *attend(q_ref[rr, 0:QBLK, :], kcat, vcat, bias_first))
        for n in range(1, n_blocks):
            rows = slice(n * QBLK, (n + 1) * QBLK)
            keys = slice((n - 1) * QBLK, (n + 1) * QBLK)
            store(rr, rows, *attend(q_ref[rr, rows, :], k_ref[rr, keys, :], v_ref[rr, keys, :], bias))


def _attention(qkv, *, rb, tq):
    _, B, d, L, _ = qkv.shape
    bpt = tq // QBLK

    def cur(which):
        return pl.BlockSpec((None, None, rb, tq, GROUP_W), lambda b, r, i: (which, b, r, i, 0))

    def prev(which):
        return pl.BlockSpec((None, None, rb, QBLK, GROUP_W),
                            lambda b, r, i: (which, b, r, jnp.maximum(i * bpt - 1, 0), 0))

    out_spec = pl.BlockSpec((None, rb, tq, GROUP_W), lambda b, r, i: (b, r, i, 0))
    return pl.pallas_call(
        functools.partial(_attn_kernel, rb=rb, tq=tq),
        grid=(B, d // rb, L // tq),
        in_specs=[cur(0), cur(1), cur(2), prev(1), prev(2)],
        out_specs=[out_spec, out_spec],
        out_shape=[jax.ShapeDtypeStruct((B, d, L, GROUP_W), BF16),
                   jax.ShapeDtypeStruct((B, d, L, GROUP_W), F32)],
        compiler_params=pltpu.CompilerParams(
            dimension_semantics=("arbitrary", "arbitrary", "arbitrary"), vmem_limit_bytes=VMEM_LIMIT_BYTES),
        name=f"attn_d{d}",
    )(qkv, qkv, qkv, qkv, qkv)


def _out_kernel(x_ref, o0_ref, l0_ref, o1_ref, l1_ref, o2_ref, l2_ref,
                yg_ref, gate_ref, wba_ref, wbg_ref, wo_ref, g_post_mix_ref, g_pre_mlp_ref, w1_ref, w2_ref,
                g_post_mlp_ref, out_ref, stage_ref, *, tm, d_model, ff_chunk):
    def token_order(refs, d, base):
        if d == 1:
            return [ref[0].astype(F32) for ref in refs]
        for r in range(d):
            rows = pl.ds(r, tm // d, stride=d)
            for a, ref in enumerate(refs):
                for s in range(N_SLABS):
                    stage_ref[base + a * N_SLABS + s, rows, :] = ref[r, :, s * LANES:(s + 1) * LANES].astype(F32)
        return [jnp.concatenate([stage_ref[base + a * N_SLABS + s] for s in range(N_SLABS)], axis=1)
                for a in range(len(refs))]

    o0, l0 = token_order((o0_ref, l0_ref), DILATIONS[0], 0)
    o1, l1 = token_order((o1_ref, l1_ref), DILATIONS[1], 0)
    o2, l2 = token_order((o2_ref, l2_ref), DILATIONS[2], 2 * N_SLABS)
    m = jnp.maximum(jnp.maximum(l0, l1), l2)
    e0, e1, e2 = jnp.exp2(l0 - m), jnp.exp2(l1 - m), jnp.exp2(l2 - m)
    y_attn = ((e0 * o0 + e1 * o1 + e2 * o2) / (e0 + e1 + e2)).astype(BF16)

    gate_a = gate_ref[:, :d_model].astype(F32)
    gate_b = gate_ref[:, d_model:].astype(F32)
    merged = (gate_a * jnp.dot(y_attn, wba_ref[...], preferred_element_type=F32)
              + gate_b * jnp.dot(yg_ref[...], wbg_ref[...], preferred_element_type=F32))
    y = jnp.dot(merged.astype(BF16), wo_ref[...], preferred_element_type=F32)
    x1 = x_ref[...] + _rms(y) * g_post_mix_ref[...]

    h = (_rms(x1) * g_pre_mlp_ref[...]).astype(BF16)
    acc = jnp.zeros((tm, d_model), F32)
    for c in range(w1_ref.shape[1] // ff_chunk):
        cols = slice(c * ff_chunk, (c + 1) * ff_chunk)
        a = jnp.maximum(jnp.dot(h, w1_ref[:, cols], preferred_element_type=F32), 0.0)
        acc = acc + jnp.dot((a * a).astype(BF16), w2_ref[cols, :], preferred_element_type=F32)
    out_ref[...] = x1 + _rms(acc) * g_post_mlp_ref[...]


def _out_proj(x, attn, yg, gates, wba, wbg, wo, g_post_mix, g_pre_mlp, w1, w2, g_post_mlp, *, tm, ff_chunk):
    B, S, D = x.shape
    d_ff = w1.shape[1]
    tok = lambda w: pl.BlockSpec((None, tm, w), lambda b, i: (b, i, 0))
    attn_args, attn_specs = [], []
    for d, parts in zip(DILATIONS, attn):
        attn_args += list(parts)
        attn_specs += [pl.BlockSpec((None, d, tm // d, GROUP_W), lambda b, i: (b, 0, i, 0))] * len(parts)
    return pl.pallas_call(
        functools.partial(_out_kernel, tm=tm, d_model=D, ff_chunk=ff_chunk),
        grid=(B, S // tm),
        in_specs=[tok(D)] + attn_specs + [
            tok(GMLP_WIDTH), tok(2 * D),
            _const_spec((GROUP_W, D)), _const_spec((GMLP_WIDTH, D)), _const_spec((D, D)),
            _const_spec((1, D)), _const_spec((1, D)),
            _const_spec((D, d_ff)), _const_spec((d_ff, D)), _const_spec((1, D)),
        ],
        out_specs=tok(D),
        out_shape=jax.ShapeDtypeStruct((B, S, D), F32),
        scratch_shapes=[pltpu.VMEM((2 * 2 * N_SLABS, tm, LANES), F32)],
        compiler_params=pltpu.CompilerParams(
            dimension_semantics=("arbitrary", "arbitrary"), vmem_limit_bytes=VMEM_LIMIT_BYTES),
        name="out_proj_mlp",
    )(x, *attn_args, yg, gates, wba, wbg, wo, g_post_mix, g_pre_mlp, w1, w2, g_post_mlp)


def _in_proj_weight(w_in):
    d_model = w_in.shape[0]
    qk = w_in[:, :2 * ATTN_WIDTH].reshape(d_model, 2 * N_GROUPS, HEADS_PER_GROUP, 2, HALF)
    qk = qk.transpose(0, 1, 3, 2, 4).reshape(d_model, 2 * ATTN_WIDTH)
    return jnp.concatenate([qk, w_in[:, 2 * ATTN_WIDTH:]], axis=1).astype(BF16)


def _rope_tables(seq):
    inv_freq = ROPE_THETA ** (-jnp.arange(HALF, dtype=F32) / HALF)
    ang = jnp.arange(seq, dtype=F32)[:, None] * inv_freq[None, :]
    reps = LANES // HALF
    return jnp.tile(jnp.cos(ang), (1, reps)), jnp.tile(jnp.sin(ang), (1, reps))


def _attn_tiling(d, L):
    tq = min(L, 2048 // min(d, 4))
    rb = min(d, 4)
    return rb, tq


def kernel(x, norm_pre_mix, w_in, w_spatial, b_spatial, ln_v_gain, ln_v_bias, w_branch_attn, w_branch_gmlp,
           w_out, norm_post_mix, norm_pre_mlp, w_mlp_in, w_mlp_out, norm_post_mlp):
    B, S, D = x.shape
    depth = w_in.shape[0]
    tm_in, tm_out = TM_IN_PROJ, TM_OUT_PROJ
    assert S % tm_in == 0 and S % tm_out == 0 and all(S % (d * QBLK) == 0 for d in DILATIONS)
    cos, sin = _rope_tables(S)
    row = lambda p: p.reshape(1, -1)
    for layer in range(depth):
        qkv0, qkv1, qkv2, yg, gates = _in_proj(
            x, row(norm_pre_mix[layer]), _in_proj_weight(w_in[layer]), cos, sin,
            w_spatial[layer], b_spatial[layer].T, row(ln_v_gain[layer]), row(ln_v_bias[layer]), tm=tm_in)
        attn = []
        for qkv in (qkv0, qkv1, qkv2):
            rb, tq = _attn_tiling(qkv.shape[2], qkv.shape[3])
            attn.append(_attention(qkv, rb=rb, tq=tq))
        x = _out_proj(
            x, attn, yg, gates,
            w_branch_attn[layer].astype(BF16), w_branch_gmlp[layer].astype(BF16), w_out[layer].astype(BF16),
            row(norm_post_mix[layer]), row(norm_pre_mlp[layer]),
            w_mlp_in[layer].astype(BF16), w_mlp_out[layer].astype(BF16), row(norm_post_mlp[layer]),
            tm=tm_out, ff_chunk=FF_CHUNK)
    return x
```

```python
import functools

import numpy as np
import jax
import jax.numpy as jnp
from jax import lax
from jax.experimental import pallas as pl
from jax.experimental.pallas import tpu as pltpu

HEAD_DIM = 64
HALF = HEAD_DIM // 2
DIL_PATTERNS = ((128, 1), (512, 4), (2048, 16))
DILATIONS = tuple(d for _, d in DIL_PATTERNS)
N_GROUPS = len(DIL_PATTERNS)
HEADS_PER_GROUP = 4
GROUP_W = HEADS_PER_GROUP * HEAD_DIM
ATTN_WIDTH = N_GROUPS * GROUP_W
QBLK = 128
ROPE_THETA = 10000.0
CHUNK = 128
GMLP_GROUPS = 4
GMLP_WIDTH = GMLP_GROUPS * 128
EPS = 1e-6
NEG = -1e30
Q_SCALE = np.float32(HEAD_DIM ** -0.5 * np.log2(np.e))

LANES = 128
N_SLABS = GROUP_W // LANES

BF16 = jnp.bfloat16
F32 = jnp.float32

VMEM_LIMIT_BYTES = 56 * 1024 * 1024
TM_IN_PROJ = 1024
TM_OUT_PROJ = 512
N_SUB_OUT_PROJ = 2
FF_CHUNK = 1024


def _rms(y):
    return y * lax.rsqrt(jnp.mean(y * y, axis=-1, keepdims=True) + EPS)


def _gelu_tanh(x):
    c = np.float32(np.sqrt(2.0 / np.pi))
    return x * (0.5 * (1.0 + jnp.tanh(c * (x + np.float32(0.044715) * (x * x * x)))))


def _sigmoid(x):
    return 1.0 / (1.0 + jnp.exp(-x))


def _const_spec(shape):
    return pl.BlockSpec(shape, lambda *_: (0,) * len(shape), pipeline_mode=pl.Buffered(1))


def _in_proj_kernel(x_ref, gain_ref, w_ref, cos_ref, sin_ref, wsp_ref, bsp_ref, lng_ref, lnb_ref,
                    qkv0_ref, qkv1_ref, qkv2_ref, yg_ref, gate_ref, stage_ref, *, tm, d_model):
    q0, k0, v0 = 0, ATTN_WIDTH, 2 * ATTN_WIDTH
    u0 = 3 * ATTN_WIDTH
    z0 = u0 + GMLP_WIDTH
    ga0 = z0 + GMLP_WIDTH

    h = (_rms(x_ref[...]) * gain_ref[...]).astype(BF16)
    cos = cos_ref[...]
    sin = sin_ref[...]

    def proj(c0, width):
        return jnp.dot(h, w_ref[:, c0:c0 + width], preferred_element_type=F32)

    def rope(t):
        t1, t2 = t[:, :LANES], t[:, LANES:]
        return t1 * cos - t2 * sin, t2 * cos + t1 * sin

    for g, (d, out_ref) in enumerate(zip(DILATIONS, (qkv0_ref, qkv1_ref, qkv2_ref))):
        q = proj(q0 + g * GROUP_W, GROUP_W) * Q_SCALE
        k = proj(k0 + g * GROUP_W, GROUP_W)
        v = proj(v0 + g * GROUP_W, GROUP_W)
        slabs = rope(q) + rope(k) + (v[:, :LANES], v[:, LANES:])
        for s, val in enumerate(slabs):
            which, cols = s // N_SLABS, slice((s % N_SLABS) * LANES, (s % N_SLABS + 1) * LANES)
            if d == 1:
                out_ref[which, 0, :, cols] = val.astype(BF16)
            else:
                stage_ref[s] = val
                for r in range(d):
                    out_ref[which, r, :, cols] = stage_ref[s, pl.ds(r, tm // d, stride=d), :].astype(BF16)

    u = _gelu_tanh(proj(u0, GMLP_WIDTH))
    z = _gelu_tanh(proj(z0, GMLP_WIDTH))
    zc = z - jnp.mean(z, axis=-1, keepdims=True)
    var = jnp.mean(zc * zc, axis=-1, keepdims=True)
    zn = (zc * lax.rsqrt(var + EPS) * lng_ref[...] + lnb_ref[...]).astype(BF16)
    n_chunks = tm // CHUNK
    pos_out = lax.broadcasted_iota(jnp.int32, (CHUNK, CHUNK), 0)
    pos_in = lax.broadcasted_iota(jnp.int32, (CHUNK, CHUNK), 1)
    causal = pos_in <= pos_out
    for g in range(GMLP_GROUPS):
        cols = slice(g * LANES, (g + 1) * LANES)
        w = jnp.where(causal, wsp_ref[g], 0.0).astype(BF16)
        zcat = jnp.concatenate([zn[c * CHUNK:(c + 1) * CHUNK, cols] for c in range(n_chunks)], axis=1)
        sz = jnp.dot(w, zcat, preferred_element_type=F32) + bsp_ref[:, g:g + 1]
        for c in range(n_chunks):
            rows = slice(c * CHUNK, (c + 1) * CHUNK)
            yg_ref[rows, cols] = (u[rows, cols] * sz[:, c * CHUNK:(c + 1) * CHUNK]).astype(BF16)

    gate_cols = 512
    for c in range(2 * d_model // gate_cols):
        gate_ref[:, c * gate_cols:(c + 1) * gate_cols] = _sigmoid(proj(ga0 + c * gate_cols, gate_cols)).astype(BF16)


def _in_proj(x, gain, w_in, cos, sin, w_sp, b_sp_t, ln_g, ln_b, *, tm):
    B, S, D = x.shape
    in_width = w_in.shape[1]
    kern = functools.partial(_in_proj_kernel, tm=tm, d_model=D)
    qkv_shapes = [jax.ShapeDtypeStruct((3, B, d, S // d, GROUP_W), BF16) for d in DILATIONS]
    qkv_specs = [pl.BlockSpec((3, None, d, tm // d, GROUP_W), lambda b, i: (0, b, 0, i, 0)) for d in DILATIONS]
    return pl.pallas_call(
        kern,
        grid=(B, S // tm),
        in_specs=[
            pl.BlockSpec((None, tm, D), lambda b, i: (b, i, 0)),
            _const_spec((1, D)),
            _const_spec((D, in_width)),
            pl.BlockSpec((tm, LANES), lambda b, i: (i, 0)),
            pl.BlockSpec((tm, LANES), lambda b, i: (i, 0)),
            _const_spec((GMLP_GROUPS, CHUNK, CHUNK)),
            _const_spec((CHUNK, GMLP_GROUPS)),
            _const_spec((1, GMLP_WIDTH)),
            _const_spec((1, GMLP_WIDTH)),
        ],
        out_specs=qkv_specs + [
            pl.BlockSpec((None, tm, GMLP_WIDTH), lambda b, i: (b, i, 0)),
            pl.BlockSpec((None, tm, 2 * D), lambda b, i: (b, i, 0)),
        ],
        out_shape=qkv_shapes + [
            jax.ShapeDtypeStruct((B, S, GMLP_WIDTH), BF16),
            jax.ShapeDtypeStruct((B, S, 2 * D), BF16),
        ],
        scratch_shapes=[pltpu.VMEM((3 * N_SLABS, tm, LANES), F32)],
        compiler_params=pltpu.CompilerParams(
            dimension_semantics=("arbitrary", "arbitrary"), vmem_limit_bytes=VMEM_LIMIT_BYTES),
        name="in_proj",
    )(x, gain, w_in, cos, sin, w_sp, b_sp_t, ln_g, ln_b)


def _attn_kernel(q_ref, k_ref, v_ref, kp_ref, vp_ref, o_ref, lse_ref, *, rb, tq):
    n_blocks = tq // QBLK
    first_tile = pl.program_id(2) == 0
    row = lax.broadcasted_iota(jnp.int32, (QBLK, 2 * QBLK), 0)
    col = lax.broadcasted_iota(jnp.int32, (QBLK, 2 * QBLK), 1)
    band = (col >= row) & (col - QBLK <= row)
    bias = jnp.where(band, 0.0, NEG).astype(F32)
    bias_first = jnp.where(band & ((col >= QBLK) | jnp.logical_not(first_tile)), 0.0, NEG).astype(F32)
    lane = lax.broadcasted_iota(jnp.int32, (QBLK, GROUP_W), 1)
    qk_head = (lane % LANES) // HALF
    heads_per_slab = LANES // HEAD_DIM
    low_head = lax.broadcasted_iota(jnp.int32, (QBLK, LANES), 1) < HEAD_DIM
    ones = jnp.ones((2 * QBLK, LANES), BF16)

    def attend(q, kcat, vcat, bias):
        qs = jnp.concatenate([jnp.where(qk_head == hh, q, jnp.zeros_like(q)) for hh in range(HEADS_PER_GROUP)],
                             axis=0)
        s = lax.dot_general(qs, kcat, (((1,), (1,)), ((), ())), preferred_element_type=F32)
        s = s.reshape(HEADS_PER_GROUP, QBLK, 2 * QBLK) + bias[None]
        m = jnp.max(s, axis=-1, keepdims=True)
        pb = jnp.exp2(s - m).astype(BF16)
        o, lse = [], []
        for t in range(N_SLABS):
            lo, hi = heads_per_slab * t, heads_per_slab * t + 1
            v_ones = jnp.concatenate([vcat[:, t * LANES:(t + 1) * LANES], ones], axis=1)
            pv = jnp.dot(pb[lo:hi + 1].reshape(heads_per_slab * QBLK, 2 * QBLK), v_ones,
                         preferred_element_type=F32)
            num = jnp.where(low_head, pv[:QBLK, :LANES], pv[QBLK:, :LANES])
            den = jnp.where(low_head, pv[:QBLK, LANES:], pv[QBLK:, LANES:])
            o.append(num * (1.0 / den))
            lse.append(jnp.where(low_head, m[lo], m[hi]) + jnp.log2(den))
        return jnp.concatenate(o, axis=1), jnp.concatenate(lse, axis=1)

    def store(rr, rows, o, lse):
        o_ref[rr, rows, :] = o.astype(BF16)
        lse_ref[rr, rows, :] = lse

    for rr in range(rb):
        kcat = jnp.concatenate([kp_ref[rr], k_ref[rr, 0:QBLK, :]], axis=0)
        vcat = jnp.concatenate([vp_ref[rr], v_ref[rr, 0:QBLK, :]], axis=0)
        store(rr, slice(0, QBLK), *attend(q_ref[rr, 0:QBLK, :], kcat, vcat, bias_first))
        for n in range(1, n_blocks):
            rows = slice(n * QBLK, (n + 1) * QBLK)
            keys = slice((n - 1) * QBLK, (n + 1) * QBLK)
            store(rr, rows, *attend(q_ref[rr, rows, :], k_ref[rr, keys, :], v_ref[rr, keys, :], bias))


def _attention(qkv, *, rb, tq):
    _, B, d, L, _ = qkv.shape
    bpt = tq // QBLK

    def cur(which):
        return pl.BlockSpec((None, None, rb, tq, GROUP_W), lambda b, r, i: (which, b, r, i, 0))

    def prev(which):
        return pl.BlockSpec((None, None, rb, QBLK, GROUP_W),
                            lambda b, r, i: (which, b, r, jnp.maximum(i * bpt - 1, 0), 0))

    out_spec = pl.BlockSpec((None, rb, tq, GROUP_W), lambda b, r, i: (b, r, i, 0))
    return pl.pallas_call(
        functools.partial(_attn_kernel, rb=rb, tq=tq),
        grid=(B, d // rb, L // tq),
        in_specs=[cur(0), cur(1), cur(2), prev(1), prev(2)],
        out_specs=[out_spec, out_spec],
        out_shape=[jax.ShapeDtypeStruct((B, d, L, GROUP_W), BF16),
                   jax.ShapeDtypeStruct((B, d, L, GROUP_W), F32)],
        compiler_params=pltpu.CompilerParams(
            dimension_semantics=("arbitrary", "arbitrary", "arbitrary"), vmem_limit_bytes=VMEM_LIMIT_BYTES),
        name=f"attn_d{d}",
    )(qkv, qkv, qkv, qkv, qkv)


def _out_kernel(x_ref, o0_ref, l0_ref, o1_ref, l1_ref, o2_ref, l2_ref, yg_ref, gate_ref,
                wba_ref, wbg_ref, wo_ref, g_post_mix_ref, g_pre_mlp_ref, w1_ref, w2_ref, g_post_mlp_ref,
                out_ref, stage_ref, *, tm, n_sub, d_model, ff_chunk):
    ts = tm // n_sub
    subs = range(n_sub)
    y_attn, merged, y, x1, h, a, acc = ({} for _ in range(7))

    def token_order(refs, g, i):
        d = DILATIONS[g]
        if d == 1:
            return [ref[0, i * ts:(i + 1) * ts, :].astype(F32) for ref in refs]
        per_class = ts // d
        base = (i * (N_GROUPS - 1) + g - 1) * len(refs) * N_SLABS
        for r in range(d):
            rows = pl.ds(r, per_class, stride=d)
            for j, ref in enumerate(refs):
                for sl in range(N_SLABS):
                    stage_ref[base + j * N_SLABS + sl, rows, :] = (
                        ref[r, i * per_class:(i + 1) * per_class, sl * LANES:(sl + 1) * LANES].astype(F32))
        return [jnp.concatenate([stage_ref[base + j * N_SLABS + sl] for sl in range(N_SLABS)], axis=1)
                for j in range(len(refs))]

    def mix_groups(i):
        o0, l0 = token_order((o0_ref, l0_ref), 0, i)
        o1, l1 = token_order((o1_ref, l1_ref), 1, i)
        o2, l2 = token_order((o2_ref, l2_ref), 2, i)
        m = jnp.maximum(jnp.maximum(l0, l1), l2)
        e0, e1, e2 = jnp.exp2(l0 - m), jnp.exp2(l1 - m), jnp.exp2(l2 - m)
        y_attn[i] = ((e0 * o0 + e1 * o1 + e2 * o2) / (e0 + e1 + e2)).astype(BF16)

    def branches(i):
        rows = slice(i * ts, (i + 1) * ts)
        gate_a = gate_ref[rows, :d_model].astype(F32)
        gate_b = gate_ref[rows, d_model:].astype(F32)
        merged[i] = (gate_a * jnp.dot(y_attn[i], wba_ref[...], preferred_element_type=F32)
                     + gate_b * jnp.dot(yg_ref[rows, :], wbg_ref[...], preferred_element_type=F32)).astype(BF16)

    def out_projection(i):
        y[i] = jnp.dot(merged[i], wo_ref[...], preferred_element_type=F32)

    def norms(i):
        x1[i] = x_ref[i * ts:(i + 1) * ts, :] + _rms(y[i]) * g_post_mix_ref[...]
        h[i] = (_rms(x1[i]) * g_pre_mlp_ref[...]).astype(BF16)
        acc[i] = jnp.zeros((ts, d_model), F32)

    def mlp_up(i, c):
        up = jnp.maximum(jnp.dot(h[i], w1_ref[:, c * ff_chunk:(c + 1) * ff_chunk],
                                 preferred_element_type=F32), 0.0)
        a[i] = (up * up).astype(BF16)

    def mlp_down(i, c):
        acc[i] = acc[i] + jnp.dot(a[i], w2_ref[c * ff_chunk:(c + 1) * ff_chunk, :], preferred_element_type=F32)

    def store(i):
        out_ref[i * ts:(i + 1) * ts, :] = x1[i] + _rms(acc[i]) * g_post_mlp_ref[...]

    mix_groups(0)
    for i in subs:
        branches(i)
        if i + 1 < n_sub:
            mix_groups(i + 1)
        out_projection(i)
    for i in subs:
        norms(i)
    for c in range(w1_ref.shape[1] // ff_chunk):
        for i in subs:
            mlp_up(i, c)
        for i in subs:
            mlp_down(i, c)
    for i in subs:
        store(i)


def _out_proj(x, attn, yg, gates, wba, wbg, wo, g_post_mix, g_pre_mlp, w1, w2, g_post_mlp, *, tm, ff_chunk):
    B, S, D = x.shape
    d_ff = w1.shape[1]
    n_sub = N_SUB_OUT_PROJ
    tok = lambda w: pl.BlockSpec((None, tm, w), lambda b, i: (b, i, 0))
    attn_args, attn_specs = [], []
    for d, parts in zip(DILATIONS, attn):
        attn_args += list(parts)
        attn_specs += [pl.BlockSpec((None, d, tm // d, GROUP_W), lambda b, i: (b, 0, i, 0))] * len(parts)
    return pl.pallas_call(
        functools.partial(_out_kernel, tm=tm, n_sub=n_sub, d_model=D, ff_chunk=ff_chunk),
        grid=(B, S // tm),
        in_specs=[tok(D)] + attn_specs + [
            tok(GMLP_WIDTH), tok(2 * D),
            _const_spec((GROUP_W, D)), _const_spec((GMLP_WIDTH, D)), _const_spec((D, D)),
            _const_spec((1, D)), _const_spec((1, D)),
            _const_spec((D, d_ff)), _const_spec((d_ff, D)), _const_spec((1, D)),
        ],
        out_specs=tok(D),
        out_shape=jax.ShapeDtypeStruct((B, S, D), F32),
        scratch_shapes=[pltpu.VMEM((n_sub * (N_GROUPS - 1) * 2 * N_SLABS, tm // n_sub, LANES), F32)],
        compiler_params=pltpu.CompilerParams(
            dimension_semantics=("arbitrary", "arbitrary"), vmem_limit_bytes=VMEM_LIMIT_BYTES),
        name="out_proj_mlp",
    )(x, *attn_args, yg, gates, wba, wbg, wo, g_post_mix, g_pre_mlp, w1, w2, g_post_mlp)


def _in_proj_weight(w_in):
    d_model = w_in.shape[0]
    qk = w_in[:, :2 * ATTN_WIDTH].reshape(d_model, 2 * N_GROUPS, HEADS_PER_GROUP, 2, HALF)
    qk = qk.transpose(0, 1, 3, 2, 4).reshape(d_model, 2 * ATTN_WIDTH)
    return jnp.concatenate([qk, w_in[:, 2 * ATTN_WIDTH:]], axis=1).astype(BF16)


def _rope_tables(seq):
    inv_freq = ROPE_THETA ** (-jnp.arange(HALF, dtype=F32) / HALF)
    ang = jnp.arange(seq, dtype=F32)[:, None] * inv_freq[None, :]
    reps = LANES // HALF
    return jnp.tile(jnp.cos(ang), (1, reps)), jnp.tile(jnp.sin(ang), (1, reps))


def _attn_tiling(d, L):
    tq = min(L, 2048 // min(d, 4))
    rb = min(d, 4)
    return rb, tq


def kernel(x, norm_pre_mix, w_in, w_spatial, b_spatial, ln_v_gain, ln_v_bias, w_branch_attn, w_branch_gmlp,
           w_out, norm_post_mix, norm_pre_mlp, w_mlp_in, w_mlp_out, norm_post_mlp):
    B, S, D = x.shape
    depth = w_in.shape[0]
    tm_in, tm_out = TM_IN_PROJ, TM_OUT_PROJ
    assert S % tm_in == 0 and S % tm_out == 0 and all(S % (d * QBLK) == 0 for d in DILATIONS)
    cos, sin = _rope_tables(S)
    row = lambda p: p.reshape(1, -1)
    for layer in range(depth):
        qkv0, qkv1, qkv2, yg, gates = _in_proj(
            x, row(norm_pre_mix[layer]), _in_proj_weight(w_in[layer]), cos, sin,
            w_spatial[layer], b_spatial[layer].T, row(ln_v_gain[layer]), row(ln_v_bias[layer]), tm=tm_in)
        attn = []
        for qkv in (qkv0, qkv1, qkv2):
            rb, tq = _attn_tiling(qkv.shape[2], qkv.shape[3])
            attn.append(_attention(qkv, rb=rb, tq=tq))
        x = _out_proj(
            x, attn, yg, gates,
            w_branch_attn[layer].astype(BF16), w_branch_gmlp[layer].astype(BF16), w_out[layer].astype(BF16),
            row(norm_post_mix[layer]), row(norm_pre_mlp[layer]),
            w_mlp_in[layer].astype(BF16), w_mlp_out[layer].astype(BF16), row(norm_post_mlp[layer]),
            tm=tm_out, ff_chunk=FF_CHUNK)
    return x
```

```python
import functools

import numpy as np
import jax
import jax.numpy as jnp
from jax import lax
from jax.experimental import pallas as pl
from jax.experimental.pallas import tpu as pltpu

HEAD_DIM = 64
HALF = HEAD_DIM // 2
DIL_PATTERNS = ((128, 1), (512, 4), (2048, 16))
DILATIONS = tuple(d for _, d in DIL_PATTERNS)
N_GROUPS = len(DIL_PATTERNS)
HEADS_PER_GROUP = 4
GROUP_W = HEADS_PER_GROUP * HEAD_DIM
ATTN_WIDTH = N_GROUPS * GROUP_W
QBLK = 128
ROPE_THETA = 10000.0
CHUNK = 128
GMLP_GROUPS = 4
GMLP_WIDTH = GMLP_GROUPS * 128
EPS = 1e-6
NEG = -1e30
Q_SCALE = np.float32(HEAD_DIM ** -0.5 * np.log2(np.e))

LANES = 128
N_SLABS = GROUP_W // LANES
MAX_SINGLE_STRIDE = 4

BF16 = jnp.bfloat16
F32 = jnp.float32

VMEM_LIMIT_BYTES = 56 * 1024 * 1024
TM_IN_PROJ = 1024
TM_OUT_PROJ = 512
N_SUB_OUT_PROJ = 2
FF_CHUNK = 1024


def _rms(y):
    return y * lax.rsqrt(jnp.mean(y * y, axis=-1, keepdims=True) + EPS)


def _gelu_tanh(x):
    c = np.sqrt(2.0 / np.pi)
    inner = x * (np.float32(c) + np.float32(c * 0.044715) * (x * x))
    return x * (0.5 + 0.5 * jnp.tanh(inner))


def _sigmoid(x):
    return 1.0 / (1.0 + jnp.exp(-x))


def _const_spec(shape):
    return pl.BlockSpec(shape, lambda *_: (0,) * len(shape), pipeline_mode=pl.Buffered(1))


def _in_proj_kernel(x_ref, gain_ref, wqk_ref, wrest_ref, cos_ref, sin_ref, wsp_ref, bsp_ref, lng_ref, lnb_ref,
                    qkv0_ref, qkv1_ref, qkv2_ref, yg_ref, gate_ref, stage_ref, half_ref, *, tm, d_model):
    q0, k0, v0 = 0, ATTN_WIDTH, 2 * ATTN_WIDTH
    u0 = 3 * ATTN_WIDTH
    z0 = u0 + GMLP_WIDTH
    ga0 = z0 + GMLP_WIDTH

    h = (_rms(x_ref[...]) * gain_ref[...]).astype(BF16)
    cos = cos_ref[...]
    sin = sin_ref[...]

    def proj(c0, width):
        w = wqk_ref[:, c0:c0 + width] if c0 < v0 else wrest_ref[:, c0 - v0:c0 - v0 + width]
        return jnp.dot(h, w, preferred_element_type=F32)

    def rope(t):
        t1, t2 = t[:, :LANES], t[:, LANES:]
        return t1 * cos - t2 * sin, t2 * cos + t1 * sin

    def attention_group(g):
        d, out_ref = DILATIONS[g], (qkv0_ref, qkv1_ref, qkv2_ref)[g]
        q = proj(q0 + g * GROUP_W, GROUP_W) * Q_SCALE
        k = proj(k0 + g * GROUP_W, GROUP_W)
        v = proj(v0 + g * GROUP_W, GROUP_W)
        slabs = rope(q) + rope(k) + (v[:, :LANES], v[:, LANES:])
        for s, val in enumerate(slabs):
            which, cols = s // N_SLABS, slice((s % N_SLABS) * LANES, (s % N_SLABS + 1) * LANES)
            if d == 1:
                out_ref[which, 0, :, cols] = val.astype(BF16)
                continue
            stage_ref[s] = val
            if d <= MAX_SINGLE_STRIDE:
                for r in range(d):
                    out_ref[which, r, :, cols] = stage_ref[s, pl.ds(r, tm // d, stride=d), :].astype(BF16)
            else:
                f = MAX_SINGLE_STRIDE
                for r0 in range(f):
                    half_ref[s * f + r0] = stage_ref[s, pl.ds(r0, tm // f, stride=f), :]
                    for r1 in range(d // f):
                        out_ref[which, r1 * f + r0, :, cols] = (
                            half_ref[s * f + r0, pl.ds(r1, tm // d, stride=d // f), :].astype(BF16))

    def gmlp_normalised_z():
        z = _gelu_tanh(proj(z0, GMLP_WIDTH))
        zc = z - jnp.mean(z, axis=-1, keepdims=True)
        var = jnp.mean(zc * zc, axis=-1, keepdims=True)
        return (zc * lax.rsqrt(var + EPS) * lng_ref[...] + lnb_ref[...]).astype(BF16)

    def gmlp_spatial_gating(u, zn):
        n_chunks = tm // CHUNK
        pos_out = lax.broadcasted_iota(jnp.int32, (CHUNK, CHUNK), 0)
        pos_in = lax.broadcasted_iota(jnp.int32, (CHUNK, CHUNK), 1)
        causal = pos_in <= pos_out
        for g in range(GMLP_GROUPS):
            cols = slice(g * LANES, (g + 1) * LANES)
            w = jnp.where(causal, wsp_ref[g], 0.0).astype(BF16)
            zcat = jnp.concatenate([zn[c * CHUNK:(c + 1) * CHUNK, cols] for c in range(n_chunks)], axis=1)
            sz = jnp.dot(w, zcat, preferred_element_type=F32) + bsp_ref[:, g:g + 1]
            for c in range(n_chunks):
                rows = slice(c * CHUNK, (c + 1) * CHUNK)
                yg_ref[rows, cols] = (u[rows, cols] * sz[:, c * CHUNK:(c + 1) * CHUNK]).astype(BF16)

    gate_cols = 512

    def gates(c):
        gate_ref[:, c * gate_cols:(c + 1) * gate_cols] = _sigmoid(proj(ga0 + c * gate_cols, gate_cols)).astype(BF16)

    attention_group(0)
    u = _gelu_tanh(proj(u0, GMLP_WIDTH))
    attention_group(1)
    zn = gmlp_normalised_z()
    attention_group(2)
    gates(0)
    gmlp_spatial_gating(u, zn)
    for c in range(1, 2 * d_model // gate_cols):
        gates(c)


def _in_proj(x, gain, w_qk, w_rest, cos, sin, w_sp, b_sp_t, ln_g, ln_b, *, tm):
    B, S, D = x.shape
    kern = functools.partial(_in_proj_kernel, tm=tm, d_model=D)
    qkv_shapes = [jax.ShapeDtypeStruct((3, B, d, S // d, GROUP_W), BF16) for d in DILATIONS]
    qkv_specs = [pl.BlockSpec((3, None, d, tm // d, GROUP_W), lambda b, i: (0, b, 0, i, 0)) for d in DILATIONS]
    return pl.pallas_call(
        kern,
        grid=(B, S // tm),
        in_specs=[
            pl.BlockSpec((None, tm, D), lambda b, i: (b, i, 0)),
            _const_spec((1, D)),
            _const_spec(w_qk.shape),
            _const_spec(w_rest.shape),
            pl.BlockSpec((tm, LANES), lambda b, i: (i, 0)),
            pl.BlockSpec((tm, LANES), lambda b, i: (i, 0)),
            _const_spec((GMLP_GROUPS, CHUNK, CHUNK)),
            _const_spec((CHUNK, GMLP_GROUPS)),
            _const_spec((1, GMLP_WIDTH)),
            _const_spec((1, GMLP_WIDTH)),
        ],
        out_specs=qkv_specs + [
            pl.BlockSpec((None, tm, GMLP_WIDTH), lambda b, i: (b, i, 0)),
            pl.BlockSpec((None, tm, 2 * D), lambda b, i: (b, i, 0)),
        ],
        out_shape=qkv_shapes + [
            jax.ShapeDtypeStruct((B, S, GMLP_WIDTH), BF16),
            jax.ShapeDtypeStruct((B, S, 2 * D), BF16),
        ],
        scratch_shapes=[pltpu.VMEM((3 * N_SLABS, tm, LANES), F32),
                        pltpu.VMEM((3 * N_SLABS * MAX_SINGLE_STRIDE, tm // MAX_SINGLE_STRIDE, LANES), F32)],
        compiler_params=pltpu.CompilerParams(
            dimension_semantics=("arbitrary", "arbitrary"), vmem_limit_bytes=VMEM_LIMIT_BYTES),
        name="in_proj",
    )(x, gain, w_qk, w_rest, cos, sin, w_sp, b_sp_t, ln_g, ln_b)


def _attn_kernel(q_ref, k_ref, v_ref, kp_ref, vp_ref, o_ref, lse_ref, *, rb, tq):
    n_blocks = tq // QBLK
    first_tile = pl.program_id(2) == 0
    row = lax.broadcasted_iota(jnp.int32, (QBLK, 2 * QBLK), 0)
    col = lax.broadcasted_iota(jnp.int32, (QBLK, 2 * QBLK), 1)
    band = (col >= row) & (col - QBLK <= row)
    bias = jnp.where(band, 0.0, NEG).astype(F32)
    bias_first = jnp.where(band & ((col >= QBLK) | jnp.logical_not(first_tile)), 0.0, NEG).astype(F32)
    lane = lax.broadcasted_iota(jnp.int32, (QBLK, GROUP_W), 1)
    qk_head = (lane % LANES) // HALF
    heads_per_slab = LANES // HEAD_DIM
    low_head = lax.broadcasted_iota(jnp.int32, (QBLK, LANES), 1) < HEAD_DIM
    ones = jnp.ones((2 * QBLK, LANES), BF16)

    def attend(q, kcat, vcat, bias):
        qs = jnp.concatenate([jnp.where(qk_head == hh, q, jnp.zeros_like(q)) for hh in range(HEADS_PER_GROUP)],
                             axis=0)
        s = lax.dot_general(qs, kcat, (((1,), (1,)), ((), ())), preferred_element_type=F32)
        s = s.reshape(HEADS_PER_GROUP, QBLK, 2 * QBLK) + bias[None]
        m = jnp.max(s, axis=-1, keepdims=True)
        pb = jnp.exp2(s - m).astype(BF16)
        o, lse = [], []
        for t in range(N_SLABS):
            lo, hi = heads_per_slab * t, heads_per_slab * t + 1
            v_ones = jnp.concatenate([vcat[:, t * LANES:(t + 1) * LANES], ones], axis=1)
            pv = jnp.dot(pb[lo:hi + 1].reshape(heads_per_slab * QBLK, 2 * QBLK), v_ones,
                         preferred_element_type=F32)
            num = jnp.where(low_head, pv[:QBLK, :LANES], pv[QBLK:, :LANES])
            den = jnp.where(low_head, pv[:QBLK, LANES:], pv[QBLK:, LANES:])
            o.append(num * (1.0 / den))
            lse.append(jnp.where(low_head, m[lo], m[hi]) + jnp.log2(den))
        return jnp.concatenate(o, axis=1), jnp.concatenate(lse, axis=1)

    def store(rr, rows, o, lse):
        o_ref[rr, rows, :] = o.astype(BF16)
        lse_ref[rr, rows, :] = lse

    for rr in range(rb):
        kcat = jnp.concatenate([kp_ref[rr], k_ref[rr, 0:QBLK, :]], axis=0)
        vcat = jnp.concatenate([vp_ref[rr], v_ref[rr, 0:QBLK, :]], axis=0)
        store(rr, slice(0, QBLK), *attend(q_ref[rr, 0:QBLK, :], kcat, vcat, bias_first))
        for n in range(1, n_blocks):
            rows = slice(n * QBLK, (n + 1) * QBLK)
            keys = slice((n - 1) * QBLK, (n + 1) * QBLK)
            store(rr, rows, *attend(q_ref[rr, rows, :], k_ref[rr, keys, :], v_ref[rr, keys, :], bias))


def _attention(qkv, *, rb, tq):
    _, B, d, L, _ = qkv.shape
    bpt = tq // QBLK

    def cur(which):
        return pl.BlockSpec((None, None, rb, tq, GROUP_W), lambda b, r, i: (which, b, r, i, 0))

    def prev(which):
        return pl.BlockSpec((None, None, rb, QBLK, GROUP_W),
                            lambda b, r, i: (which, b, r, jnp.maximum(i * bpt - 1, 0), 0))

    out_spec = pl.BlockSpec((None, rb, tq, GROUP_W), lambda b, r, i: (b, r, i, 0))
    return pl.pallas_call(
        functools.partial(_attn_kernel, rb=rb, tq=tq),
        grid=(B, d // rb, L // tq),
        in_specs=[cur(0), cur(1), cur(2), prev(1), prev(2)],
        out_specs=[out_spec, out_spec],
        out_shape=[jax.ShapeDtypeStruct((B, d, L, GROUP_W), BF16),
                   jax.ShapeDtypeStruct((B, d, L, GROUP_W), F32)],
        compiler_params=pltpu.CompilerParams(
            dimension_semantics=("arbitrary", "arbitrary", "arbitrary"), vmem_limit_bytes=VMEM_LIMIT_BYTES),
        name=f"attn_d{d}",
    )(qkv, qkv, qkv, qkv, qkv)


def _out_kernel(x_ref, o0_ref, l0_ref, o1_ref, l1_ref, o2_ref, l2_ref, yg_ref, gate_ref,
                wba_ref, wbg_ref, wo_ref, g_post_mix_ref, g_pre_mlp_ref, w1_ref, w2_ref, g_post_mlp_ref,
                out_ref, stage_ref, half_ref, *, tm, n_sub, d_model, ff_chunk):
    ts = tm // n_sub
    subs = range(n_sub)
    y_attn, merged, y, x1, h, a, acc = ({} for _ in range(7))

    def token_order(refs, g, i):
        d = DILATIONS[g]
        if d == 1:
            return [ref[0, i * ts:(i + 1) * ts, :].astype(F32) for ref in refs]
        per_class = ts // d
        n_slabs = len(refs) * N_SLABS
        base = (i * (N_GROUPS - 1) + g - 1) * n_slabs

        def class_rows(r, j, sl):
            return refs[j][r, i * per_class:(i + 1) * per_class, sl * LANES:(sl + 1) * LANES].astype(F32)

        if d <= MAX_SINGLE_STRIDE:
            for r in range(d):
                for j in range(len(refs)):
                    for sl in range(N_SLABS):
                        stage_ref[base + j * N_SLABS + sl, pl.ds(r, per_class, stride=d), :] = class_rows(r, j, sl)
        else:
            f = MAX_SINGLE_STRIDE
            for r0 in range(f):
                for j in range(len(refs)):
                    for sl in range(N_SLABS):
                        slab = j * N_SLABS + sl
                        half = (i * f + r0) * n_slabs + slab
                        for r1 in range(d // f):
                            half_ref[half, pl.ds(r1, per_class, stride=d // f), :] = class_rows(r1 * f + r0, j, sl)
                        stage_ref[base + slab, pl.ds(r0, ts // f, stride=f), :] = half_ref[half]
        return [jnp.concatenate([stage_ref[base + j * N_SLABS + sl] for sl in range(N_SLABS)], axis=1)
                for j in range(len(refs))]

    def mix_groups(i):
        o0, l0 = token_order((o0_ref, l0_ref), 0, i)
        o1, l1 = token_order((o1_ref, l1_ref), 1, i)
        o2, l2 = token_order((o2_ref, l2_ref), 2, i)
        m = jnp.maximum(jnp.maximum(l0, l1), l2)
        e0, e1, e2 = jnp.exp2(l0 - m), jnp.exp2(l1 - m), jnp.exp2(l2 - m)
        y_attn[i] = ((e0 * o0 + e1 * o1 + e2 * o2) / (e0 + e1 + e2)).astype(BF16)

    def branches(i):
        rows = slice(i * ts, (i + 1) * ts)
        gate_a = gate_ref[rows, :d_model].astype(F32)
        gate_b = gate_ref[rows, d_model:].astype(F32)
        merged[i] = (gate_a * jnp.dot(y_attn[i], wba_ref[...], preferred_element_type=F32)
                     + gate_b * jnp.dot(yg_ref[rows, :], wbg_ref[...], preferred_element_type=F32)).astype(BF16)

    def out_projection(i):
        y[i] = jnp.dot(merged[i], wo_ref[...], preferred_element_type=F32)

    def norms(i):
        x1[i] = x_ref[i * ts:(i + 1) * ts, :] + _rms(y[i]) * g_post_mix_ref[...]
        h[i] = (_rms(x1[i]) * g_pre_mlp_ref[...]).astype(BF16)
        acc[i] = jnp.zeros((ts, d_model), F32)

    def mlp_up(i, c):
        up = jnp.maximum(jnp.dot(h[i], w1_ref[:, c * ff_chunk:(c + 1) * ff_chunk],
                                 preferred_element_type=F32), 0.0)
        a[i] = (up * up).astype(BF16)

    def mlp_down(i, c):
        acc[i] = acc[i] + jnp.dot(a[i], w2_ref[c * ff_chunk:(c + 1) * ff_chunk, :], preferred_element_type=F32)

    def store(i):
        out_ref[i * ts:(i + 1) * ts, :] = x1[i] + _rms(acc[i]) * g_post_mlp_ref[...]

    mix_groups(0)
    for i in subs:
        branches(i)
        if i + 1 < n_sub:
            mix_groups(i + 1)
        out_projection(i)
    for i in subs:
        norms(i)
    for c in range(w1_ref.shape[1] // ff_chunk):
        for i in subs:
            mlp_up(i, c)
        for i in subs:
            mlp_down(i, c)
    for i in subs:
        store(i)


def _out_proj(x, attn, yg, gates, wba, wbg, wo, g_post_mix, g_pre_mlp, w1, w2, g_post_mlp, *, tm, ff_chunk):
    B, S, D = x.shape
    d_ff = w1.shape[1]
    n_sub = N_SUB_OUT_PROJ
    tok = lambda w: pl.BlockSpec((None, tm, w), lambda b, i: (b, i, 0))
    attn_args, attn_specs = [], []
    for d, parts in zip(DILATIONS, attn):
        attn_args += list(parts)
        attn_specs += [pl.BlockSpec((None, d, tm // d, GROUP_W), lambda b, i: (b, 0, i, 0))] * len(parts)
    return pl.pallas_call(
        functools.partial(_out_kernel, tm=tm, n_sub=n_sub, d_model=D, ff_chunk=ff_chunk),
        grid=(B, S // tm),
        in_specs=[tok(D)] + attn_specs + [
            tok(GMLP_WIDTH), tok(2 * D),
            _const_spec((GROUP_W, D)), _const_spec((GMLP_WIDTH, D)), _const_spec((D, D)),
            _const_spec((1, D)), _const_spec((1, D)),
            _const_spec((D, d_ff)), _const_spec((d_ff, D)), _const_spec((1, D)),
        ],
        out_specs=tok(D),
        out_shape=jax.ShapeDtypeStruct((B, S, D), F32),
        scratch_shapes=[
            pltpu.VMEM((n_sub * (N_GROUPS - 1) * 2 * N_SLABS, tm // n_sub, LANES), F32),
            pltpu.VMEM((n_sub * MAX_SINGLE_STRIDE * 2 * N_SLABS, tm // n_sub // MAX_SINGLE_STRIDE, LANES), F32)],
        compiler_params=pltpu.CompilerParams(
            dimension_semantics=("arbitrary", "arbitrary"), vmem_limit_bytes=VMEM_LIMIT_BYTES),
        name="out_proj_mlp",
    )(x, *attn_args, yg, gates, wba, wbg, wo, g_post_mix, g_pre_mlp, w1, w2, g_post_mlp)


def _in_proj_weights(w_in):
    d_model = w_in.shape[0]
    qk = w_in[:, :2 * ATTN_WIDTH].reshape(d_model, 2 * N_GROUPS, HEADS_PER_GROUP, 2, HALF)
    qk = qk.transpose(0, 1, 3, 2, 4).reshape(d_model, 2 * ATTN_WIDTH)
    return qk.astype(BF16), w_in[:, 2 * ATTN_WIDTH:].astype(BF16)


def _rope_tables(seq):
    inv_freq = ROPE_THETA ** (-jnp.arange(HALF, dtype=F32) / HALF)
    ang = jnp.arange(seq, dtype=F32)[:, None] * inv_freq[None, :]
    reps = LANES // HALF
    return jnp.tile(jnp.cos(ang), (1, reps)), jnp.tile(jnp.sin(ang), (1, reps))


def _attn_tiling(d, L):
    tq = min(L, 2048 // min(d, 4))
    rb = min(d, 4)
    return rb, tq


def kernel(x, norm_pre_mix, w_in, w_spatial, b_spatial, ln_v_gain, ln_v_bias, w_branch_attn, w_branch_gmlp,
           w_out, norm_post_mix, norm_pre_mlp, w_mlp_in, w_mlp_out, norm_post_mlp):
    B, S, D = x.shape
    depth = w_in.shape[0]
    tm_in, tm_out = TM_IN_PROJ, TM_OUT_PROJ
    assert S % tm_in == 0 and S % tm_out == 0 and all(S % (d * QBLK) == 0 for d in DILATIONS)
    cos, sin = _rope_tables(S)
    row = lambda p: p.reshape(1, -1)
    for layer in range(depth):
        qkv0, qkv1, qkv2, yg, gates = _in_proj(
            x, row(norm_pre_mix[layer]), *_in_proj_weights(w_in[layer]), cos, sin,
            w_spatial[layer], b_spatial[layer].T, row(ln_v_gain[layer]), row(ln_v_bias[layer]), tm=tm_in)
        attn = []
        for qkv in (qkv0, qkv1, qkv2):
            rb, tq = _attn_tiling(qkv.shape[2], qkv.shape[3])
            attn.append(_attention(qkv, rb=rb, tq=tq))
        x = _out_proj(
            x, attn, yg, gates,
            w_branch_attn[layer].astype(BF16), w_branch_gmlp[layer].astype(BF16), w_out[layer].astype(BF16),
            row(norm_post_mix[layer]), row(norm_pre_mlp[layer]),
            w_mlp_in[layer].astype(BF16), w_mlp_out[layer].astype(BF16), row(norm_post_mlp[layer]),
            tm=tm_out, ff_chunk=FF_CHUNK)
    return x
```

```python
import functools

import numpy as np
import jax
import jax.numpy as jnp
from jax import lax
from jax.experimental import pallas as pl
from jax.experimental.pallas import tpu as pltpu

HEAD_DIM = 64
HALF = HEAD_DIM // 2
DIL_PATTERNS = ((128, 1), (512, 4), (2048, 16))
DILATIONS = tuple(d for _, d in DIL_PATTERNS)
N_GROUPS = len(DIL_PATTERNS)
HEADS_PER_GROUP = 4
GROUP_W = HEADS_PER_GROUP * HEAD_DIM
ATTN_WIDTH = N_GROUPS * GROUP_W
QBLK = 128
ROPE_THETA = 10000.0
CHUNK = 128
GMLP_GROUPS = 4
GMLP_WIDTH = GMLP_GROUPS * 128
EPS = 1e-6
NEG = -1e30
Q_SCALE = np.float32(HEAD_DIM ** -0.5 * np.log2(np.e))

LANES = 128
N_SLABS = GROUP_W // LANES
MAX_SINGLE_STRIDE = 4

BF16 = jnp.bfloat16
F32 = jnp.float32

VMEM_LIMIT_BYTES = 56 * 1024 * 1024
TM_IN_PROJ = 1024
TM_OUT_PROJ = 512
ATTN_TOKENS_PER_STEP = 4096
N_SUB_OUT_PROJ = 2
FF_CHUNK = 1024


def _rms(y):
    return y * lax.rsqrt(jnp.mean(y * y, axis=-1, keepdims=True) + EPS)


def _gelu_tanh(x):
    c = np.sqrt(2.0 / np.pi)
    inner = x * (np.float32(c) + np.float32(c * 0.044715) * (x * x))
    return x * (0.5 + 0.5 * jnp.tanh(inner))


def _sigmoid(x):
    return 1.0 / (1.0 + jnp.exp(-x))


def _const_spec(shape):
    return pl.BlockSpec(shape, lambda *_: (0,) * len(shape), pipeline_mode=pl.Buffered(1))


def _in_proj_kernel(x_ref, gain_ref, wqk_ref, wrest_ref, cos_ref, sin_ref, wsp_ref, bsp_ref, lng_ref, lnb_ref,
                    qkv0_ref, qkv1_ref, qkv2_ref, yg_ref, gate_ref, stage_ref, half_ref, *, tm, d_model):
    q0, k0, v0 = 0, ATTN_WIDTH, 2 * ATTN_WIDTH
    u0 = 3 * ATTN_WIDTH
    z0 = u0 + GMLP_WIDTH
    ga0 = z0 + GMLP_WIDTH

    h = (_rms(x_ref[...]) * gain_ref[...]).astype(BF16)
    cos = cos_ref[...]
    sin = sin_ref[...]

    def proj(c0, width):
        w = wqk_ref[:, c0:c0 + width] if c0 < v0 else wrest_ref[:, c0 - v0:c0 - v0 + width]
        return jnp.dot(h, w, preferred_element_type=F32)

    def rope(t):
        t1, t2 = t[:, :LANES], t[:, LANES:]
        return t1 * cos - t2 * sin, t2 * cos + t1 * sin

    def attention_group(g):
        d, out_ref = DILATIONS[g], (qkv0_ref, qkv1_ref, qkv2_ref)[g]
        q = proj(q0 + g * GROUP_W, GROUP_W) * Q_SCALE
        k = proj(k0 + g * GROUP_W, GROUP_W)
        v = proj(v0 + g * GROUP_W, GROUP_W)
        slabs = rope(q) + rope(k) + (v[:, :LANES], v[:, LANES:])
        for s, val in enumerate(slabs):
            which, cols = s // N_SLABS, slice((s % N_SLABS) * LANES, (s % N_SLABS + 1) * LANES)
            if d == 1:
                out_ref[which, 0, :, cols] = val.astype(BF16)
                continue
            stage_ref[s] = val
            if d <= MAX_SINGLE_STRIDE:
                for r in range(d):
                    out_ref[which, r, :, cols] = stage_ref[s, pl.ds(r, tm // d, stride=d), :].astype(BF16)
            else:
                f = MAX_SINGLE_STRIDE
                for r0 in range(f):
                    half_ref[s * f + r0] = stage_ref[s, pl.ds(r0, tm // f, stride=f), :]
                    for r1 in range(d // f):
                        out_ref[which, r1 * f + r0, :, cols] = (
                            half_ref[s * f + r0, pl.ds(r1, tm // d, stride=d // f), :].astype(BF16))

    def gmlp_normalised_z():
        z = _gelu_tanh(proj(z0, GMLP_WIDTH))
        zc = z - jnp.mean(z, axis=-1, keepdims=True)
        var = jnp.mean(zc * zc, axis=-1, keepdims=True)
        return (zc * lax.rsqrt(var + EPS) * lng_ref[...] + lnb_ref[...]).astype(BF16)

    def gmlp_spatial_gating(u, zn):
        n_chunks = tm // CHUNK
        pos_out = lax.broadcasted_iota(jnp.int32, (CHUNK, CHUNK), 0)
        pos_in = lax.broadcasted_iota(jnp.int32, (CHUNK, CHUNK), 1)
        causal = pos_in <= pos_out
        for g in range(GMLP_GROUPS):
            cols = slice(g * LANES, (g + 1) * LANES)
            w = jnp.where(causal, wsp_ref[g], 0.0).astype(BF16)
            zcat = jnp.concatenate([zn[c * CHUNK:(c + 1) * CHUNK, cols] for c in range(n_chunks)], axis=1)
            sz = jnp.dot(w, zcat, preferred_element_type=F32) + bsp_ref[:, g:g + 1]
            for c in range(n_chunks):
                rows = slice(c * CHUNK, (c + 1) * CHUNK)
                yg_ref[rows, cols] = (u[rows, cols] * sz[:, c * CHUNK:(c + 1) * CHUNK]).astype(BF16)

    gate_cols = 512

    def gates(c):
        gate_ref[:, c * gate_cols:(c + 1) * gate_cols] = _sigmoid(proj(ga0 + c * gate_cols, gate_cols)).astype(BF16)

    attention_group(0)
    u = _gelu_tanh(proj(u0, GMLP_WIDTH))
    attention_group(1)
    zn = gmlp_normalised_z()
    attention_group(2)
    gates(0)
    gmlp_spatial_gating(u, zn)
    for c in range(1, 2 * d_model // gate_cols):
        gates(c)


def _in_proj(x, gain, w_qk, w_rest, cos, sin, w_sp, b_sp_t, ln_g, ln_b, *, tm):
    B, S, D = x.shape
    kern = functools.partial(_in_proj_kernel, tm=tm, d_model=D)
    qkv_shapes = [jax.ShapeDtypeStruct((3, B, d, S // d, GROUP_W), BF16) for d in DILATIONS]
    qkv_specs = [pl.BlockSpec((3, None, d, tm // d, GROUP_W), lambda b, i: (0, b, 0, i, 0)) for d in DILATIONS]
    return pl.pallas_call(
        kern,
        grid=(B, S // tm),
        in_specs=[
            pl.BlockSpec((None, tm, D), lambda b, i: (b, i, 0)),
            _const_spec((1, D)),
            _const_spec(w_qk.shape),
            _const_spec(w_rest.shape),
            pl.BlockSpec((tm, LANES), lambda b, i: (i, 0)),
            pl.BlockSpec((tm, LANES), lambda b, i: (i, 0)),
            _const_spec((GMLP_GROUPS, CHUNK, CHUNK)),
            _const_spec((CHUNK, GMLP_GROUPS)),
            _const_spec((1, GMLP_WIDTH)),
            _const_spec((1, GMLP_WIDTH)),
        ],
        out_specs=qkv_specs + [
            pl.BlockSpec((None, tm, GMLP_WIDTH), lambda b, i: (b, i, 0)),
            pl.BlockSpec((None, tm, 2 * D), lambda b, i: (b, i, 0)),
        ],
        out_shape=qkv_shapes + [
            jax.ShapeDtypeStruct((B, S, GMLP_WIDTH), BF16),
            jax.ShapeDtypeStruct((B, S, 2 * D), BF16),
        ],
        scratch_shapes=[pltpu.VMEM((3 * N_SLABS, tm, LANES), F32),
                        pltpu.VMEM((3 * N_SLABS * MAX_SINGLE_STRIDE, tm // MAX_SINGLE_STRIDE, LANES), F32)],
        compiler_params=pltpu.CompilerParams(
            dimension_semantics=("arbitrary", "arbitrary"), vmem_limit_bytes=VMEM_LIMIT_BYTES),
        name="in_proj",
    )(x, gain, w_qk, w_rest, cos, sin, w_sp, b_sp_t, ln_g, ln_b)


def _attn_kernel(q_ref, k_ref, v_ref, kp_ref, vp_ref, o_ref, lse_ref, *, rb, tq):
    n_blocks = tq // QBLK
    first_tile = pl.program_id(2) == 0
    row = lax.broadcasted_iota(jnp.int32, (QBLK, 2 * QBLK), 0)
    col = lax.broadcasted_iota(jnp.int32, (QBLK, 2 * QBLK), 1)
    band = (col >= row) & (col - QBLK <= row)
    bias = jnp.where(band, 0.0, NEG).astype(F32)
    bias_first = jnp.where(band & ((col >= QBLK) | jnp.logical_not(first_tile)), 0.0, NEG).astype(F32)
    lane = lax.broadcasted_iota(jnp.int32, (QBLK, GROUP_W), 1)
    qk_head = (lane % LANES) // HALF
    heads_per_slab = LANES // HEAD_DIM
    low_head = lax.broadcasted_iota(jnp.int32, (QBLK, LANES), 1) < HEAD_DIM
    ones = jnp.ones((2 * QBLK, LANES), BF16)

    def attend(q, kcat, vcat, bias):
        qs = jnp.concatenate([jnp.where(qk_head == hh, q, jnp.zeros_like(q)) for hh in range(HEADS_PER_GROUP)],
                             axis=0)
        s = lax.dot_general(qs, kcat, (((1,), (1,)), ((), ())), preferred_element_type=F32)
        s = s.reshape(HEADS_PER_GROUP, QBLK, 2 * QBLK) + bias[None]
        m = jnp.max(s, axis=-1, keepdims=True)
        pb = jnp.exp2(s - m).astype(BF16)
        o, lse = [], []
        for t in range(N_SLABS):
            lo, hi = heads_per_slab * t, heads_per_slab * t + 1
            v_ones = jnp.concatenate([vcat[:, t * LANES:(t + 1) * LANES], ones], axis=1)
            pv = jnp.dot(pb[lo:hi + 1].reshape(heads_per_slab * QBLK, 2 * QBLK), v_ones,
                         preferred_element_type=F32)
            num = jnp.where(low_head, pv[:QBLK, :LANES], pv[QBLK:, :LANES])
            den = jnp.where(low_head, pv[:QBLK, LANES:], pv[QBLK:, LANES:])
            o.append(num * (1.0 / den))
            lse.append(jnp.where(low_head, m[lo], m[hi]) + jnp.log2(den))
        return jnp.concatenate(o, axis=1), jnp.concatenate(lse, axis=1)

    def store(rr, rows, o, lse):
        o_ref[rr, rows, :] = o.astype(BF16)
        lse_ref[rr, rows, :] = lse

    for rr in range(rb):
        kcat = jnp.concatenate([kp_ref[rr], k_ref[rr, 0:QBLK, :]], axis=0)
        vcat = jnp.concatenate([vp_ref[rr], v_ref[rr, 0:QBLK, :]], axis=0)
        store(rr, slice(0, QBLK), *attend(q_ref[rr, 0:QBLK, :], kcat, vcat, bias_first))
        for n in range(1, n_blocks):
            rows = slice(n * QBLK, (n + 1) * QBLK)
            keys = slice((n - 1) * QBLK, (n + 1) * QBLK)
            store(rr, rows, *attend(q_ref[rr, rows, :], k_ref[rr, keys, :], v_ref[rr, keys, :], bias))


def _attention(qkv, *, rb, tq):
    _, B, d, L, _ = qkv.shape
    bpt = tq // QBLK

    def cur(which):
        return pl.BlockSpec((None, None, rb, tq, GROUP_W), lambda b, r, i: (which, b, r, i, 0))

    def prev(which):
        return pl.BlockSpec((None, None, rb, QBLK, GROUP_W),
                            lambda b, r, i: (which, b, r, jnp.maximum(i * bpt - 1, 0), 0))

    out_spec = pl.BlockSpec((None, rb, tq, GROUP_W), lambda b, r, i: (b, r, i, 0))
    return pl.pallas_call(
        functools.partial(_attn_kernel, rb=rb, tq=tq),
        grid=(B, d // rb, L // tq),
        in_specs=[cur(0), cur(1), cur(2), prev(1), prev(2)],
        out_specs=[out_spec, out_spec],
        out_shape=[jax.ShapeDtypeStruct((B, d, L, GROUP_W), BF16),
                   jax.ShapeDtypeStruct((B, d, L, GROUP_W), F32)],
        compiler_params=pltpu.CompilerParams(
            dimension_semantics=("arbitrary", "arbitrary", "arbitrary"), vmem_limit_bytes=VMEM_LIMIT_BYTES),
        name=f"attn_d{d}",
    )(qkv, qkv, qkv, qkv, qkv)


def _out_kernel(x_ref, o0_ref, l0_ref, o1_ref, l1_ref, o2_ref, l2_ref, yg_ref, gate_ref,
                wba_ref, wbg_ref, wo_ref, g_post_mix_ref, g_pre_mlp_ref, w1_ref, w2_ref, g_post_mlp_ref,
                out_ref, stage_ref, half_ref, *, tm, n_sub, d_model, ff_chunk):
    ts = tm // n_sub
    subs = range(n_sub)
    y_attn, merged, y, x1, h, a, acc = ({} for _ in range(7))

    def token_order(refs, g, i):
        d = DILATIONS[g]
        if d == 1:
            return [ref[0, i * ts:(i + 1) * ts, :].astype(F32) for ref in refs]
        per_class = ts // d
        n_slabs = len(refs) * N_SLABS
        base = (i * (N_GROUPS - 1) + g - 1) * n_slabs

        def class_rows(r, j, sl):
            return refs[j][r, i * per_class:(i + 1) * per_class, sl * LANES:(sl + 1) * LANES].astype(F32)

        if d <= MAX_SINGLE_STRIDE:
            for r in range(d):
                for j in range(len(refs)):
                    for sl in range(N_SLABS):
                        stage_ref[base + j * N_SLABS + sl, pl.ds(r, per_class, stride=d), :] = class_rows(r, j, sl)
        else:
            f = MAX_SINGLE_STRIDE
            for r0 in range(f):
                for j in range(len(refs)):
                    for sl in range(N_SLABS):
                        slab = j * N_SLABS + sl
                        half = (i * f + r0) * n_slabs + slab
                        for r1 in range(d // f):
                            half_ref[half, pl.ds(r1, per_class, stride=d // f), :] = class_rows(r1 * f + r0, j, sl)
                        stage_ref[base + slab, pl.ds(r0, ts // f, stride=f), :] = half_ref[half]
        return [jnp.concatenate([stage_ref[base + j * N_SLABS + sl] for sl in range(N_SLABS)], axis=1)
                for j in range(len(refs))]

    def mix_groups(i):
        o0, l0 = token_order((o0_ref, l0_ref), 0, i)
        o1, l1 = token_order((o1_ref, l1_ref), 1, i)
        o2, l2 = token_order((o2_ref, l2_ref), 2, i)
        m = jnp.maximum(jnp.maximum(l0, l1), l2)
        e0, e1, e2 = jnp.exp2(l0 - m), jnp.exp2(l1 - m), jnp.exp2(l2 - m)
        y_attn[i] = ((e0 * o0 + e1 * o1 + e2 * o2) / (e0 + e1 + e2)).astype(BF16)

    def branches(i):
        rows = slice(i * ts, (i + 1) * ts)
        gate_a = gate_ref[rows, :d_model].astype(F32)
        gate_b = gate_ref[rows, d_model:].astype(F32)
        merged[i] = (gate_a * jnp.dot(y_attn[i], wba_ref[...], preferred_element_type=F32)
                     + gate_b * jnp.dot(yg_ref[rows, :], wbg_ref[...], preferred_element_type=F32)).astype(BF16)

    def out_projection(i):
        y[i] = jnp.dot(merged[i], wo_ref[...], preferred_element_type=F32)

    def norms(i):
        x1[i] = x_ref[i * ts:(i + 1) * ts, :] + _rms(y[i]) * g_post_mix_ref[...]
        h[i] = (_rms(x1[i]) * g_pre_mlp_ref[...]).astype(BF16)
        acc[i] = jnp.zeros((ts, d_model), F32)

    def mlp_up(i, c):
        up = jnp.maximum(jnp.dot(h[i], w1_ref[:, c * ff_chunk:(c + 1) * ff_chunk],
                                 preferred_element_type=F32), 0.0)
        a[i] = (up * up).astype(BF16)

    def mlp_down(i, c):
        acc[i] = acc[i] + jnp.dot(a[i], w2_ref[c * ff_chunk:(c + 1) * ff_chunk, :], preferred_element_type=F32)

    def store(i):
        out_ref[i * ts:(i + 1) * ts, :] = x1[i] + _rms(acc[i]) * g_post_mlp_ref[...]

    mix_groups(0)
    for i in subs:
        branches(i)
        if i + 1 < n_sub:
            mix_groups(i + 1)
        out_projection(i)
    for i in subs:
        norms(i)
    for c in range(w1_ref.shape[1] // ff_chunk):
        for i in subs:
            mlp_up(i, c)
        for i in subs:
            mlp_down(i, c)
    for i in subs:
        store(i)


def _out_proj(x, attn, yg, gates, wba, wbg, wo, g_post_mix, g_pre_mlp, w1, w2, g_post_mlp, *, tm, ff_chunk):
    B, S, D = x.shape
    d_ff = w1.shape[1]
    n_sub = N_SUB_OUT_PROJ
    tok = lambda w: pl.BlockSpec((None, tm, w), lambda b, i: (b, i, 0))
    attn_args, attn_specs = [], []
    for d, parts in zip(DILATIONS, attn):
        attn_args += list(parts)
        attn_specs += [pl.BlockSpec((None, d, tm // d, GROUP_W), lambda b, i: (b, 0, i, 0))] * len(parts)
    return pl.pallas_call(
        functools.partial(_out_kernel, tm=tm, n_sub=n_sub, d_model=D, ff_chunk=ff_chunk),
        grid=(B, S // tm),
        in_specs=[tok(D)] + attn_specs + [
            tok(GMLP_WIDTH), tok(2 * D),
            _const_spec((GROUP_W, D)), _const_spec((GMLP_WIDTH, D)), _const_spec((D, D)),
            _const_spec((1, D)), _const_spec((1, D)),
            _const_spec((D, d_ff)), _const_spec((d_ff, D)), _const_spec((1, D)),
        ],
        out_specs=tok(D),
        out_shape=jax.ShapeDtypeStruct((B, S, D), F32),
        scratch_shapes=[
            pltpu.VMEM((n_sub * (N_GROUPS - 1) * 2 * N_SLABS, tm // n_sub, LANES), F32),
            pltpu.VMEM((n_sub * MAX_SINGLE_STRIDE * 2 * N_SLABS, tm // n_sub // MAX_SINGLE_STRIDE, LANES), F32)],
        compiler_params=pltpu.CompilerParams(
            dimension_semantics=("arbitrary", "arbitrary"), vmem_limit_bytes=VMEM_LIMIT_BYTES),
        name="out_proj_mlp",
    )(x, *attn_args, yg, gates, wba, wbg, wo, g_post_mix, g_pre_mlp, w1, w2, g_post_mlp)


def _in_proj_weights(w_in):
    d_model = w_in.shape[0]
    qk = w_in[:, :2 * ATTN_WIDTH].reshape(d_model, 2 * N_GROUPS, HEADS_PER_GROUP, 2, HALF)
    qk = qk.transpose(0, 1, 3, 2, 4).reshape(d_model, 2 * ATTN_WIDTH)
    return qk.astype(BF16), w_in[:, 2 * ATTN_WIDTH:].astype(BF16)


def _rope_tables(seq):
    inv_freq = ROPE_THETA ** (-(jnp.arange(LANES) % HALF).astype(F32) / HALF)
    ang = jnp.arange(seq, dtype=F32)[:, None] * inv_freq[None, :]
    return jnp.cos(ang), jnp.sin(ang)


def _attn_tiling(d, L):
    tq = min(L, max(ATTN_TOKENS_PER_STEP // d, 4 * QBLK))
    rb = min(d, ATTN_TOKENS_PER_STEP // tq)
    return rb, tq


def kernel(x, norm_pre_mix, w_in, w_spatial, b_spatial, ln_v_gain, ln_v_bias, w_branch_attn, w_branch_gmlp,
           w_out, norm_post_mix, norm_pre_mlp, w_mlp_in, w_mlp_out, norm_post_mlp):
    B, S, D = x.shape
    depth = w_in.shape[0]
    tm_in, tm_out = TM_IN_PROJ, TM_OUT_PROJ
    assert S % tm_in == 0 and S % tm_out == 0 and all(S % (d * QBLK) == 0 for d in DILATIONS)
    cos, sin = _rope_tables(S)
    row = lambda p: p.reshape(1, -1)
    for layer in range(depth):
        qkv0, qkv1, qkv2, yg, gates = _in_proj(
            x, row(norm_pre_mix[layer]), *_in_proj_weights(w_in[layer]), cos, sin,
            w_spatial[layer], b_spatial[layer].T, row(ln_v_gain[layer]), row(ln_v_bias[layer]), tm=tm_in)
        attn = []
        for qkv in (qkv0, qkv1, qkv2):
            rb, tq = _attn_tiling(qkv.shape[2], qkv.shape[3])
            attn.append(_attention(qkv, rb=rb, tq=tq))
        x = _out_proj(
            x, attn, yg, gates,
            w_branch_attn[layer].astype(BF16), w_branch_gmlp[layer].astype(BF16), w_out[layer].astype(BF16),
            row(norm_post_mix[layer]), row(norm_pre_mlp[layer]),
            w_mlp_in[layer].astype(BF16), w_mlp_out[layer].astype(BF16), row(norm_post_mlp[layer]),
            tm=tm_out, ff_chunk=FF_CHUNK)
    return x
```

```python
import functools

import numpy as np
import jax
import jax.numpy as jnp
from jax import lax
from jax.experimental import pallas as pl
from jax.experimental.pallas import tpu as pltpu

HEAD_DIM = 64
HALF = HEAD_DIM // 2
DIL_PATTERNS = ((128, 1), (512, 4), (2048, 16))
DILATIONS = tuple(d for _, d in DIL_PATTERNS)
N_GROUPS = len(DIL_PATTERNS)
HEADS_PER_GROUP = 4
GROUP_W = HEADS_PER_GROUP * HEAD_DIM
ATTN_WIDTH = N_GROUPS * GROUP_W
QBLK = 128
ROPE_THETA = 10000.0
CHUNK = 128
GMLP_GROUPS = 4
GMLP_WIDTH = GMLP_GROUPS * 128
EPS = 1e-6
NEG = -1e30
Q_SCALE = np.float32(HEAD_DIM ** -0.5 * np.log2(np.e))

LANES = 128
N_SLABS = GROUP_W // LANES
MAX_SINGLE_STRIDE = 4

BF16 = jnp.bfloat16
F32 = jnp.float32

VMEM_LIMIT_BYTES = 56 * 1024 * 1024
TM_IN_PROJ = 1024
TM_OUT_PROJ = 512
ATTN_TOKENS_PER_STEP = 4096
N_SUB_OUT_PROJ = 2
FF_CHUNK = 1024


def _rms(y):
    return y * lax.rsqrt(jnp.mean(y * y, axis=-1, keepdims=True) + EPS)


def _gelu_tanh(x):
    c = np.sqrt(2.0 / np.pi)
    inner = x * (np.float32(c) + np.float32(c * 0.044715) * (x * x))
    return x * (0.5 + 0.5 * jnp.tanh(inner))


def _sigmoid(x):
    return 1.0 / (1.0 + jnp.exp(-x))


def _const_spec(shape):
    return pl.BlockSpec(shape, lambda *_: (0,) * len(shape), pipeline_mode=pl.Buffered(1))


def _in_proj_kernel(x_ref, gain_ref, wqk_ref, wrest_ref, cos_ref, sin_ref, wsp_ref, bsp_ref, lng_ref, lnb_ref,
                    qkv0_ref, qkv1_ref, qkv2_ref, yg_ref, gate_ref, stage_ref, half_ref, *, tm, d_model):
    q0, k0, v0 = 0, ATTN_WIDTH, 2 * ATTN_WIDTH
    u0 = 3 * ATTN_WIDTH
    z0 = u0 + GMLP_WIDTH
    ga0 = z0 + GMLP_WIDTH

    h = (_rms(x_ref[...]) * gain_ref[...]).astype(BF16)
    cos = cos_ref[...]
    sin = sin_ref[...]

    def proj(c0, width):
        w = wqk_ref[:, c0:c0 + width] if c0 < v0 else wrest_ref[:, c0 - v0:c0 - v0 + width]
        return jnp.dot(h, w, preferred_element_type=F32)

    def rope(t):
        t1, t2 = t[:, :LANES], t[:, LANES:]
        return t1 * cos - t2 * sin, t2 * cos + t1 * sin

    def attention_group(g, qk, v):
        d, out_ref = DILATIONS[g], (qkv0_ref, qkv1_ref, qkv2_ref)[g]
        q = qk[:, q0 + g * GROUP_W:q0 + (g + 1) * GROUP_W] * Q_SCALE
        k = qk[:, k0 + g * GROUP_W:k0 + (g + 1) * GROUP_W]
        v = v[:, g * GROUP_W:(g + 1) * GROUP_W]
        slabs = rope(q) + rope(k) + (v[:, :LANES], v[:, LANES:])
        for s, val in enumerate(slabs):
            which, cols = s // N_SLABS, slice((s % N_SLABS) * LANES, (s % N_SLABS + 1) * LANES)
            if d == 1:
                out_ref[which, 0, :, cols] = val.astype(BF16)
                continue
            stage_ref[s] = val
            if d <= MAX_SINGLE_STRIDE:
                for r in range(d):
                    out_ref[which, r, :, cols] = stage_ref[s, pl.ds(r, tm // d, stride=d), :].astype(BF16)
            else:
                f = MAX_SINGLE_STRIDE
                for r0 in range(f):
                    half_ref[s * f + r0] = stage_ref[s, pl.ds(r0, tm // f, stride=f), :]
                    for r1 in range(d // f):
                        out_ref[which, r1 * f + r0, :, cols] = (
                            half_ref[s * f + r0, pl.ds(r1, tm // d, stride=d // f), :].astype(BF16))

    def gmlp_normalised(z):
        zc = z - jnp.mean(z, axis=-1, keepdims=True)
        var = jnp.mean(zc * zc, axis=-1, keepdims=True)
        return (zc * lax.rsqrt(var + EPS) * lng_ref[...] + lnb_ref[...]).astype(BF16)

    def gmlp_spatial_gating(u, zn):
        n_chunks = tm // CHUNK
        pos_out = lax.broadcasted_iota(jnp.int32, (CHUNK, CHUNK), 0)
        pos_in = lax.broadcasted_iota(jnp.int32, (CHUNK, CHUNK), 1)
        causal = pos_in <= pos_out
        for g in range(GMLP_GROUPS):
            cols = slice(g * LANES, (g + 1) * LANES)
            w = jnp.where(causal, wsp_ref[g], 0.0).astype(BF16)
            zcat = jnp.concatenate([zn[c * CHUNK:(c + 1) * CHUNK, cols] for c in range(n_chunks)], axis=1)
            sz = jnp.dot(w, zcat, preferred_element_type=F32) + bsp_ref[:, g:g + 1]
            for c in range(n_chunks):
                rows = slice(c * CHUNK, (c + 1) * CHUNK)
                yg_ref[rows, cols] = (u[rows, cols] * sz[:, c * CHUNK:(c + 1) * CHUNK]).astype(BF16)

    qk = proj(q0, 2 * ATTN_WIDTH)
    v = proj(v0, ATTN_WIDTH)
    for g in range(N_GROUPS):
        attention_group(g, qk, v)
    uz = _gelu_tanh(proj(u0, 2 * GMLP_WIDTH))
    zn = gmlp_normalised(uz[:, GMLP_WIDTH:])
    gate_ref[...] = _sigmoid(proj(ga0, 2 * d_model)).astype(BF16)
    gmlp_spatial_gating(uz[:, :GMLP_WIDTH], zn)


def _in_proj(x, gain, w_qk, w_rest, cos, sin, w_sp, b_sp_t, ln_g, ln_b, *, tm):
    B, S, D = x.shape
    kern = functools.partial(_in_proj_kernel, tm=tm, d_model=D)
    qkv_shapes = [jax.ShapeDtypeStruct((3, B, d, S // d, GROUP_W), BF16) for d in DILATIONS]
    qkv_specs = [pl.BlockSpec((3, None, d, tm // d, GROUP_W), lambda b, i: (0, b, 0, i, 0)) for d in DILATIONS]
    return pl.pallas_call(
        kern,
        grid=(B, S // tm),
        in_specs=[
            pl.BlockSpec((None, tm, D), lambda b, i: (b, i, 0)),
            _const_spec((1, D)),
            _const_spec(w_qk.shape),
            _const_spec(w_rest.shape),
            pl.BlockSpec((tm, LANES), lambda b, i: (i, 0)),
            pl.BlockSpec((tm, LANES), lambda b, i: (i, 0)),
            _const_spec((GMLP_GROUPS, CHUNK, CHUNK)),
            _const_spec((CHUNK, GMLP_GROUPS)),
            _const_spec((1, GMLP_WIDTH)),
            _const_spec((1, GMLP_WIDTH)),
        ],
        out_specs=qkv_specs + [
            pl.BlockSpec((None, tm, GMLP_WIDTH), lambda b, i: (b, i, 0)),
            pl.BlockSpec((None, tm, 2 * D), lambda b, i: (b, i, 0)),
        ],
        out_shape=qkv_shapes + [
            jax.ShapeDtypeStruct((B, S, GMLP_WIDTH), BF16),
            jax.ShapeDtypeStruct((B, S, 2 * D), BF16),
        ],
        scratch_shapes=[pltpu.VMEM((3 * N_SLABS, tm, LANES), F32),
                        pltpu.VMEM((3 * N_SLABS * MAX_SINGLE_STRIDE, tm // MAX_SINGLE_STRIDE, LANES), F32)],
        compiler_params=pltpu.CompilerParams(
            dimension_semantics=("arbitrary", "arbitrary"), vmem_limit_bytes=VMEM_LIMIT_BYTES),
        name="in_proj",
    )(x, gain, w_qk, w_rest, cos, sin, w_sp, b_sp_t, ln_g, ln_b)


def _attn_kernel(q_ref, k_ref, v_ref, kp_ref, vp_ref, o_ref, lse_ref, *, rb, tq):
    n_blocks = tq // QBLK
    first_tile = pl.program_id(2) == 0
    row = lax.broadcasted_iota(jnp.int32, (QBLK, 2 * QBLK), 0)
    col = lax.broadcasted_iota(jnp.int32, (QBLK, 2 * QBLK), 1)
    band = (col >= row) & (col - QBLK <= row)
    bias = jnp.where(band, 0.0, NEG).astype(F32)
    bias_first = jnp.where(band & ((col >= QBLK) | jnp.logical_not(first_tile)), 0.0, NEG).astype(F32)
    lane = lax.broadcasted_iota(jnp.int32, (QBLK, GROUP_W), 1)
    qk_head = (lane % LANES) // HALF
    heads_per_slab = LANES // HEAD_DIM
    low_head = lax.broadcasted_iota(jnp.int32, (QBLK, LANES), 1) < HEAD_DIM
    ones = jnp.ones((2 * QBLK, LANES), BF16)

    def attend(q, kcat, vcat, bias):
        qs = jnp.concatenate([jnp.where(qk_head == hh, q, jnp.zeros_like(q)) for hh in range(HEADS_PER_GROUP)],
                             axis=0)
        s = lax.dot_general(qs, kcat, (((1,), (1,)), ((), ())), preferred_element_type=F32)
        s = s.reshape(HEADS_PER_GROUP, QBLK, 2 * QBLK) + bias[None]
        m = jnp.max(s, axis=-1, keepdims=True)
        pb = jnp.exp2(s - m).astype(BF16)
        o, lse = [], []
        for t in range(N_SLABS):
            lo, hi = heads_per_slab * t, heads_per_slab * t + 1
            v_ones = jnp.concatenate([vcat[:, t * LANES:(t + 1) * LANES], ones], axis=1)
            pv = jnp.dot(pb[lo:hi + 1].reshape(heads_per_slab * QBLK, 2 * QBLK), v_ones,
                         preferred_element_type=F32)
            num = jnp.where(low_head, pv[:QBLK, :LANES], pv[QBLK:, :LANES])
            den = jnp.where(low_head, pv[:QBLK, LANES:], pv[QBLK:, LANES:])
            o.append(num * (1.0 / den))
            lse.append(jnp.where(low_head, m[lo], m[hi]) + jnp.log2(den))
        return jnp.concatenate(o, axis=1), jnp.concatenate(lse, axis=1)

    def store(rr, rows, o, lse):
        o_ref[rr, rows, :] = o.astype(BF16)
        lse_ref[rr, rows, :] = lse

    for rr in range(rb):
        kcat = jnp.concatenate([kp_ref[rr], k_ref[rr, 0:QBLK, :]], axis=0)
        vcat = jnp.concatenate([vp_ref[rr], v_ref[rr, 0:QBLK, :]], axis=0)
        store(rr, slice(0, QBLK), *attend(q_ref[rr, 0:QBLK, :], kcat, vcat, bias_first))
        for n in range(1, n_blocks):
            rows = slice(n * QBLK, (n + 1) * QBLK)
            keys = slice((n - 1) * QBLK, (n + 1) * QBLK)
            store(rr, rows, *attend(q_ref[rr, rows, :], k_ref[rr, keys, :], v_ref[rr, keys, :], bias))


def _attention(qkv, *, rb, tq):
    _, B, d, L, _ = qkv.shape
    bpt = tq // QBLK

    def cur(which):
        return pl.BlockSpec((None, None, rb, tq, GROUP_W), lambda b, r, i: (which, b, r, i, 0))

    def prev(which):
        return pl.BlockSpec((None, None, rb, QBLK, GROUP_W),
                            lambda b, r, i: (which, b, r, jnp.maximum(i * bpt - 1, 0), 0))

    out_spec = pl.BlockSpec((None, rb, tq, GROUP_W), lambda b, r, i: (b, r, i, 0))
    return pl.pallas_call(
        functools.partial(_attn_kernel, rb=rb, tq=tq),
        grid=(B, d // rb, L // tq),
        in_specs=[cur(0), cur(1), cur(2), prev(1), prev(2)],
        out_specs=[out_spec, out_spec],
        out_shape=[jax.ShapeDtypeStruct((B, d, L, GROUP_W), BF16),
                   jax.ShapeDtypeStruct((B, d, L, GROUP_W), F32)],
        compiler_params=pltpu.CompilerParams(
            dimension_semantics=("arbitrary", "arbitrary", "arbitrary"), vmem_limit_bytes=VMEM_LIMIT_BYTES),
        name=f"attn_d{d}",
    )(qkv, qkv, qkv, qkv, qkv)


def _out_kernel(x_ref, o0_ref, l0_ref, o1_ref, l1_ref, o2_ref, l2_ref, yg_ref, gate_ref,
                wba_ref, wbg_ref, wo_ref, g_post_mix_ref, g_pre_mlp_ref, w1_ref, w2_ref, g_post_mlp_ref,
                out_ref, stage_ref, half_ref, *, tm, n_sub, d_model, ff_chunk):
    ts = tm // n_sub
    subs = range(n_sub)
    y_attn, merged, y, x1, h, a, acc = ({} for _ in range(7))

    def token_order(refs, g, i):
        d = DILATIONS[g]
        if d == 1:
            return [ref[0, i * ts:(i + 1) * ts, :].astype(F32) for ref in refs]
        per_class = ts // d
        n_slabs = len(refs) * N_SLABS
        base = (i * (N_GROUPS - 1) + g - 1) * n_slabs

        def class_rows(r, j, sl):
            return refs[j][r, i * per_class:(i + 1) * per_class, sl * LANES:(sl + 1) * LANES].astype(F32)

        if d <= MAX_SINGLE_STRIDE:
            for r in range(d):
                for j in range(len(refs)):
                    for sl in range(N_SLABS):
                        stage_ref[base + j * N_SLABS + sl, pl.ds(r, per_class, stride=d), :] = class_rows(r, j, sl)
        else:
            f = MAX_SINGLE_STRIDE
            for r0 in range(f):
                for j in range(len(refs)):
                    for sl in range(N_SLABS):
                        slab = j * N_SLABS + sl
                        half = (i * f + r0) * n_slabs + slab
                        for r1 in range(d // f):
                            half_ref[half, pl.ds(r1, per_class, stride=d // f), :] = class_rows(r1 * f + r0, j, sl)
                        stage_ref[base + slab, pl.ds(r0, ts // f, stride=f), :] = half_ref[half]
        return [jnp.concatenate([stage_ref[base + j * N_SLABS + sl] for sl in range(N_SLABS)], axis=1)
                for j in range(len(refs))]

    def mix_groups(i):
        o0, l0 = token_order((o0_ref, l0_ref), 0, i)
        o1, l1 = token_order((o1_ref, l1_ref), 1, i)
        o2, l2 = token_order((o2_ref, l2_ref), 2, i)
        m = jnp.maximum(jnp.maximum(l0, l1), l2)
        e0, e1, e2 = jnp.exp2(l0 - m), jnp.exp2(l1 - m), jnp.exp2(l2 - m)
        y_attn[i] = ((e0 * o0 + e1 * o1 + e2 * o2) / (e0 + e1 + e2)).astype(BF16)

    def branches(i):
        rows = slice(i * ts, (i + 1) * ts)
        gate_a = gate_ref[rows, :d_model].astype(F32)
        gate_b = gate_ref[rows, d_model:].astype(F32)
        merged[i] = (gate_a * jnp.dot(y_attn[i], wba_ref[...], preferred_element_type=F32)
                     + gate_b * jnp.dot(yg_ref[rows, :], wbg_ref[...], preferred_element_type=F32)).astype(BF16)

    def out_projection(i):
        y[i] = jnp.dot(merged[i], wo_ref[...], preferred_element_type=F32)

    def norms(i):
        x1[i] = x_ref[i * ts:(i + 1) * ts, :] + _rms(y[i]) * g_post_mix_ref[...]
        h[i] = (_rms(x1[i]) * g_pre_mlp_ref[...]).astype(BF16)
        acc[i] = jnp.zeros((ts, d_model), F32)

    def mlp_up(i, c):
        up = jnp.maximum(jnp.dot(h[i], w1_ref[:, c * ff_chunk:(c + 1) * ff_chunk],
                                 preferred_element_type=F32), 0.0)
        a[i] = (up * up).astype(BF16)

    def mlp_down(i, c):
        acc[i] = acc[i] + jnp.dot(a[i], w2_ref[c * ff_chunk:(c + 1) * ff_chunk, :], preferred_element_type=F32)

    def store(i):
        out_ref[i * ts:(i + 1) * ts, :] = x1[i] + _rms(acc[i]) * g_post_mlp_ref[...]

    mix_groups(0)
    for i in subs:
        branches(i)
        if i + 1 < n_sub:
            mix_groups(i + 1)
        out_projection(i)
    for i in subs:
        norms(i)
    for c in range(w1_ref.shape[1] // ff_chunk):
        for i in subs:
            mlp_up(i, c)
        for i in subs:
            mlp_down(i, c)
    for i in subs:
        store(i)


def _out_proj(x, attn, yg, gates, wba, wbg, wo, g_post_mix, g_pre_mlp, w1, w2, g_post_mlp, *, tm, ff_chunk):
    B, S, D = x.shape
    d_ff = w1.shape[1]
    n_sub = N_SUB_OUT_PROJ
    tok = lambda w: pl.BlockSpec((None, tm, w), lambda b, i: (b, i, 0))
    attn_args, attn_specs = [], []
    for d, parts in zip(DILATIONS, attn):
        attn_args += list(parts)
        attn_specs += [pl.BlockSpec((None, d, tm // d, GROUP_W), lambda b, i: (b, 0, i, 0))] * len(parts)
    return pl.pallas_call(
        functools.partial(_out_kernel, tm=tm, n_sub=n_sub, d_model=D, ff_chunk=ff_chunk),
        grid=(B, S // tm),
        in_specs=[tok(D)] + attn_specs + [
            tok(GMLP_WIDTH), tok(2 * D),
            _const_spec((GROUP_W, D)), _const_spec((GMLP_WIDTH, D)), _const_spec((D, D)),
            _const_spec((1, D)), _const_spec((1, D)),
            _const_spec((D, d_ff)), _const_spec((d_ff, D)), _const_spec((1, D)),
        ],
        out_specs=tok(D),
        out_shape=jax.ShapeDtypeStruct((B, S, D), F32),
        scratch_shapes=[
            pltpu.VMEM((n_sub * (N_GROUPS - 1) * 2 * N_SLABS, tm // n_sub, LANES), F32),
            pltpu.VMEM((n_sub * MAX_SINGLE_STRIDE * 2 * N_SLABS, tm // n_sub // MAX_SINGLE_STRIDE, LANES), F32)],
        compiler_params=pltpu.CompilerParams(
            dimension_semantics=("arbitrary", "arbitrary"), vmem_limit_bytes=VMEM_LIMIT_BYTES),
        name="out_proj_mlp",
    )(x, *attn_args, yg, gates, wba, wbg, wo, g_post_mix, g_pre_mlp, w1, w2, g_post_mlp)


def _in_proj_weights(w_in):
    d_model = w_in.shape[0]
    qk = w_in[:, :2 * ATTN_WIDTH].reshape(d_model, 2 * N_GROUPS, HEADS_PER_GROUP, 2, HALF)
    qk = qk.transpose(0, 1, 3, 2, 4).reshape(d_model, 2 * ATTN_WIDTH)
    return qk.astype(BF16), w_in[:, 2 * ATTN_WIDTH:].astype(BF16)


def _rope_tables(seq):
    inv_freq = ROPE_THETA ** (-jnp.arange(HALF, dtype=F32) / HALF)
    ang = jnp.arange(seq, dtype=F32)[:, None] * inv_freq[None, :]
    reps = LANES // HALF
    return jnp.tile(jnp.cos(ang), (1, reps)), jnp.tile(jnp.sin(ang), (1, reps))


def _attn_tiling(d, L):
    tq = min(L, max(ATTN_TOKENS_PER_STEP // d, 4 * QBLK))
    rb = min(d, ATTN_TOKENS_PER_STEP // tq)
    return rb, tq


def kernel(x, norm_pre_mix, w_in, w_spatial, b_spatial, ln_v_gain, ln_v_bias, w_branch_attn, w_branch_gmlp,
           w_out, norm_post_mix, norm_pre_mlp, w_mlp_in, w_mlp_out, norm_post_mlp):
    B, S, D = x.shape
    depth = w_in.shape[0]
    tm_in, tm_out = TM_IN_PROJ, TM_OUT_PROJ
    assert S % tm_in == 0 and S % tm_out == 0 and all(S % (d * QBLK) == 0 for d in DILATIONS)
    cos, sin = _rope_tables(S)
    row = lambda p: p.reshape(1, -1)
    for layer in range(depth):
        qkv0, qkv1, qkv2, yg, gates = _in_proj(
            x, row(norm_pre_mix[layer]), *_in_proj_weights(w_in[layer]), cos, sin,
            w_spatial[layer], b_spatial[layer].T, row(ln_v_gain[layer]), row(ln_v_bias[layer]), tm=tm_in)
        attn = []
        for qkv in (qkv0, qkv1, qkv2):
            rb, tq = _attn_tiling(qkv.shape[2], qkv.shape[3])
            attn.append(_attention(qkv, rb=rb, tq=tq))
        x = _out_proj(
            x, attn, yg, gates,
            w_branch_attn[layer].astype(BF16), w_branch_gmlp[layer].astype(BF16), w_out[layer].astype(BF16),
            row(norm_post_mix[layer]), row(norm_pre_mlp[layer]),
            w_mlp_in[layer].astype(BF16), w_mlp_out[layer].astype(BF16), row(norm_post_mlp[layer]),
            tm=tm_out, ff_chunk=FF_CHUNK)
    return x
```

```python
import functools

import numpy as np
import jax
import jax.numpy as jnp
from jax import lax
from jax.experimental import pallas as pl
from jax.experimental.pallas import tpu as pltpu

HEAD_DIM = 64
HALF = HEAD_DIM // 2
DIL_PATTERNS = ((128, 1), (512, 4), (2048, 16))
DILATIONS = tuple(d for _, d in DIL_PATTERNS)
N_GROUPS = len(DIL_PATTERNS)
HEADS_PER_GROUP = 4
GROUP_W = HEADS_PER_GROUP * HEAD_DIM
ATTN_WIDTH = N_GROUPS * GROUP_W
QBLK = 128
ROPE_THETA = 10000.0
CHUNK = 128
GMLP_GROUPS = 4
GMLP_GROUP_CH = 128
GMLP_WIDTH = GMLP_GROUPS * GMLP_GROUP_CH
EPS = 1e-6
NEG = -1e30
Q_SCALE = np.float32(HEAD_DIM ** -0.5 * np.log2(np.e))

LANES = 128
N_SLABS = GROUP_W // LANES
MAX_SINGLE_STRIDE = 4

BF16 = jnp.bfloat16
F32 = jnp.float32

VMEM_LIMIT_BYTES = 56 * 1024 * 1024
TM_IN_PROJ = 1024
TM_OUT_PROJ = 512
ATTN_TOKENS_PER_STEP = 8192
N_SUB_OUT_PROJ = 2
FF_CHUNK = 1024


def _rms(y):
    return y * lax.rsqrt(jnp.mean(y * y, axis=-1, keepdims=True) + EPS)


def _gelu_tanh(x):
    c = np.sqrt(2.0 / np.pi)
    inner = x * (np.float32(c) + np.float32(c * 0.044715) * (x * x))
    return x * (0.5 + 0.5 * jnp.tanh(inner))


def _sigmoid(x):
    return 1.0 / (1.0 + jnp.exp(-x))


def _const_spec(shape):
    return pl.BlockSpec(shape, lambda *_: (0,) * len(shape), pipeline_mode=pl.Buffered(1))


def _in_proj_kernel(x_ref, gain_ref, wqk_ref, wrest_ref, cos_ref, sin_ref, wsp_ref, bsp_ref, lng_ref, lnb_ref,
                    qkv0_ref, qkv1_ref, qkv2_ref, yg_ref, gate_ref, stage_ref, half_ref, *, tm, d_model):
    q0, k0, v0 = 0, ATTN_WIDTH, 2 * ATTN_WIDTH
    u0 = 3 * ATTN_WIDTH
    z0 = u0 + GMLP_WIDTH
    ga0 = z0 + GMLP_WIDTH

    h = (_rms(x_ref[...]) * gain_ref[...]).astype(BF16)
    cos = cos_ref[...]
    sin = sin_ref[...]

    def proj(c0, width):
        w = wqk_ref[:, c0:c0 + width] if c0 < v0 else wrest_ref[:, c0 - v0:c0 - v0 + width]
        return jnp.dot(h, w, preferred_element_type=F32)

    def rope(t):
        t1, t2 = t[:, :LANES], t[:, LANES:]
        return t1 * cos - t2 * sin, t2 * cos + t1 * sin

    def attention_group(g, qk, v):
        d, out_ref = DILATIONS[g], (qkv0_ref, qkv1_ref, qkv2_ref)[g]
        q = qk[:, q0 + g * GROUP_W:q0 + (g + 1) * GROUP_W] * Q_SCALE
        k = qk[:, k0 + g * GROUP_W:k0 + (g + 1) * GROUP_W]
        v = v[:, g * GROUP_W:(g + 1) * GROUP_W]
        slabs = rope(q) + rope(k) + (v[:, :LANES], v[:, LANES:])
        for s, val in enumerate(slabs):
            which, cols = s // N_SLABS, slice((s % N_SLABS) * LANES, (s % N_SLABS + 1) * LANES)
            if d == 1:
                out_ref[which, 0, :, cols] = val.astype(BF16)
                continue
            stage_ref[s] = val
            if d <= MAX_SINGLE_STRIDE:
                for r in range(d):
                    out_ref[which, r, :, cols] = stage_ref[s, pl.ds(r, tm // d, stride=d), :].astype(BF16)
            else:
                f = MAX_SINGLE_STRIDE
                for r0 in range(f):
                    half_ref[s * f + r0] = stage_ref[s, pl.ds(r0, tm // f, stride=f), :]
                    for r1 in range(d // f):
                        out_ref[which, r1 * f + r0, :, cols] = (
                            half_ref[s * f + r0, pl.ds(r1, tm // d, stride=d // f), :].astype(BF16))

    def gmlp_normalised(z):
        zc = z - jnp.mean(z, axis=-1, keepdims=True)
        var = jnp.mean(zc * zc, axis=-1, keepdims=True)
        return (zc * lax.rsqrt(var + EPS) * lng_ref[...] + lnb_ref[...]).astype(BF16)

    def gmlp_spatial_gating(u, zn):
        n_chunks = tm // CHUNK
        pos_out = lax.broadcasted_iota(jnp.int32, (CHUNK, CHUNK), 0)
        pos_in = lax.broadcasted_iota(jnp.int32, (CHUNK, CHUNK), 1)
        causal = pos_in <= pos_out
        for g in range(GMLP_GROUPS):
            cols = slice(g * LANES, (g + 1) * LANES)
            w = jnp.where(causal, wsp_ref[g], 0.0).astype(BF16)
            zcat = jnp.concatenate([zn[c * CHUNK:(c + 1) * CHUNK, cols] for c in range(n_chunks)], axis=1)
            sz = jnp.dot(w, zcat, preferred_element_type=F32) + bsp_ref[:, g:g + 1]
            for c in range(n_chunks):
                rows = slice(c * CHUNK, (c + 1) * CHUNK)
                yg_ref[rows, cols] = (u[rows, cols] * sz[:, c * CHUNK:(c + 1) * CHUNK]).astype(BF16)

    qk = proj(q0, 2 * ATTN_WIDTH)
    v = proj(v0, ATTN_WIDTH)
    for g in range(N_GROUPS):
        attention_group(g, qk, v)
    uz = _gelu_tanh(proj(u0, 2 * GMLP_WIDTH))
    zn = gmlp_normalised(uz[:, GMLP_WIDTH:])
    gate_ref[...] = _sigmoid(proj(ga0, 2 * d_model)).astype(BF16)
    gmlp_spatial_gating(uz[:, :GMLP_WIDTH], zn)


def _in_proj(x, gain, w_qk, w_rest, cos, sin, w_sp, b_sp_t, ln_g, ln_b, *, tm):
    B, S, D = x.shape
    kern = functools.partial(_in_proj_kernel, tm=tm, d_model=D)
    qkv_shapes = [jax.ShapeDtypeStruct((3, B, d, S // d, GROUP_W), BF16) for d in DILATIONS]
    qkv_specs = [pl.BlockSpec((3, None, d, tm // d, GROUP_W), lambda b, i: (0, b, 0, i, 0)) for d in DILATIONS]
    return pl.pallas_call(
        kern,
        grid=(B, S // tm),
        in_specs=[
            pl.BlockSpec((None, tm, D), lambda b, i: (b, i, 0)),
            _const_spec((1, D)),
            _const_spec(w_qk.shape),
            _const_spec(w_rest.shape),
            pl.BlockSpec((tm, LANES), lambda b, i: (i, 0)),
            pl.BlockSpec((tm, LANES), lambda b, i: (i, 0)),
            _const_spec((GMLP_GROUPS, CHUNK, CHUNK)),
            _const_spec((CHUNK, GMLP_GROUPS)),
            _const_spec((1, GMLP_WIDTH)),
            _const_spec((1, GMLP_WIDTH)),
        ],
        out_specs=qkv_specs + [
            pl.BlockSpec((None, tm, GMLP_WIDTH), lambda b, i: (b, i, 0)),
            pl.BlockSpec((None, tm, 2 * D), lambda b, i: (b, i, 0)),
        ],
        out_shape=qkv_shapes + [
            jax.ShapeDtypeStruct((B, S, GMLP_WIDTH), BF16),
            jax.ShapeDtypeStruct((B, S, 2 * D), BF16),
        ],
        scratch_shapes=[pltpu.VMEM((3 * N_SLABS, tm, LANES), F32),
                        pltpu.VMEM((3 * N_SLABS * MAX_SINGLE_STRIDE, tm // MAX_SINGLE_STRIDE, LANES), F32)],
        compiler_params=pltpu.CompilerParams(
            dimension_semantics=("arbitrary", "arbitrary"), vmem_limit_bytes=VMEM_LIMIT_BYTES),
        name="in_proj",
    )(x, gain, w_qk, w_rest, cos, sin, w_sp, b_sp_t, ln_g, ln_b)


def _attn_kernel(q_ref, k_ref, v_ref, kp_ref, vp_ref, o_ref, lse_ref, *, rb, tq):
    n_blocks = tq // QBLK
    first_tile = pl.program_id(2) == 0
    row = lax.broadcasted_iota(jnp.int32, (QBLK, 2 * QBLK), 0)
    col = lax.broadcasted_iota(jnp.int32, (QBLK, 2 * QBLK), 1)
    band = (col >= row) & (col - QBLK <= row)
    bias = jnp.where(band, 0.0, NEG).astype(F32)
    bias_first = jnp.where(band & ((col >= QBLK) | jnp.logical_not(first_tile)), 0.0, NEG).astype(F32)
    lane = lax.broadcasted_iota(jnp.int32, (QBLK, GROUP_W), 1)
    qk_head = (lane % LANES) // HALF
    heads_per_slab = LANES // HEAD_DIM
    low_head = lax.broadcasted_iota(jnp.int32, (QBLK, LANES), 1) < HEAD_DIM
    ones = jnp.ones((2 * QBLK, LANES), BF16)

    def attend(q, kcat, vcat, bias):
        qs = jnp.concatenate([jnp.where(qk_head == hh, q, jnp.zeros_like(q)) for hh in range(HEADS_PER_GROUP)],
                             axis=0)
        s = lax.dot_general(qs, kcat, (((1,), (1,)), ((), ())), preferred_element_type=F32)
        s = s.reshape(HEADS_PER_GROUP, QBLK, 2 * QBLK) + bias[None]
        m = jnp.max(s, axis=-1, keepdims=True)
        pb = jnp.exp2(s - m).astype(BF16)
        o, lse = [], []
        for t in range(N_SLABS):
            lo, hi = heads_per_slab * t, heads_per_slab * t + 1
            v_ones = jnp.concatenate([vcat[:, t * LANES:(t + 1) * LANES], ones], axis=1)
            pv = jnp.dot(pb[lo:hi + 1].reshape(heads_per_slab * QBLK, 2 * QBLK), v_ones,
                         preferred_element_type=F32)
            num = jnp.where(low_head, pv[:QBLK, :LANES], pv[QBLK:, :LANES])
            den = jnp.where(low_head, pv[:QBLK, LANES:], pv[QBLK:, LANES:])
            o.append(num * (1.0 / den))
            lse.append(jnp.where(low_head, m[lo], m[hi]) + jnp.log2(den))
        return jnp.concatenate(o, axis=1), jnp.concatenate(lse, axis=1)

    def store(rr, rows, o, lse):
        o_ref[rr, rows, :] = o.astype(BF16)
        lse_ref[rr, rows, :] = lse

    for rr in range(rb):
        kcat = jnp.concatenate([kp_ref[rr], k_ref[rr, 0:QBLK, :]], axis=0)
        vcat = jnp.concatenate([vp_ref[rr], v_ref[rr, 0:QBLK, :]], axis=0)
        store(rr, slice(0, QBLK), *attend(q_ref[rr, 0:QBLK, :], kcat, vcat, bias_first))
        for n in range(1, n_blocks):
            rows = slice(n * QBLK, (n + 1) * QBLK)
            keys = slice((n - 1) * QBLK, (n + 1) * QBLK)
            store(rr, rows, *attend(q_ref[rr, rows, :], k_ref[rr, keys, :], v_ref[rr, keys, :], bias))


def _attention(qkv, *, rb, tq):
    _, B, d, L, _ = qkv.shape
    bpt = tq // QBLK

    def cur(which):
        return pl.BlockSpec((None, None, rb, tq, GROUP_W), lambda b, r, i: (which, b, r, i, 0))

    def prev(which):
        return pl.BlockSpec((None, None, rb, QBLK, GROUP_W),
                            lambda b, r, i: (which, b, r, jnp.maximum(i * bpt - 1, 0), 0))

    out_spec = pl.BlockSpec((None, rb, tq, GROUP_W), lambda b, r, i: (b, r, i, 0))
    return pl.pallas_call(
        functools.partial(_attn_kernel, rb=rb, tq=tq),
        grid=(B, d // rb, L // tq),
        in_specs=[cur(0), cur(1), cur(2), prev(1), prev(2)],
        out_specs=[out_spec, out_spec],
        out_shape=[jax.ShapeDtypeStruct((B, d, L, GROUP_W), BF16),
                   jax.ShapeDtypeStruct((B, d, L, GROUP_W), F32)],
        compiler_params=pltpu.CompilerParams(
            dimension_semantics=("arbitrary", "arbitrary", "arbitrary"), vmem_limit_bytes=VMEM_LIMIT_BYTES),
        name=f"attn_d{d}",
    )(qkv, qkv, qkv, qkv, qkv)


def _out_kernel(x_ref, o0_ref, l0_ref, o1_ref, l1_ref, o2_ref, l2_ref, yg_ref, gate_ref,
                wba_ref, wbg_ref, wo_ref, g_post_mix_ref, g_pre_mlp_ref, w1_ref, w2_ref, g_post_mlp_ref,
                out_ref, stage_ref, half_ref, *, tm, n_sub, d_model, ff_chunk):
    ts = tm // n_sub
    subs = range(n_sub)
    y_attn, merged, y, x1, h, a, acc = ({} for _ in range(7))

    def token_order(refs, g, i):
        d = DILATIONS[g]
        if d == 1:
            return [ref[0, i * ts:(i + 1) * ts, :].astype(F32) for ref in refs]
        per_class = ts // d
        n_slabs = len(refs) * N_SLABS
        base = (i * (N_GROUPS - 1) + g - 1) * n_slabs

        def class_rows(r, j, sl):
            return refs[j][r, i * per_class:(i + 1) * per_class, sl * LANES:(sl + 1) * LANES].astype(F32)

        if d <= MAX_SINGLE_STRIDE:
            for r in range(d):
                for j in range(len(refs)):
                    for sl in range(N_SLABS):
                        stage_ref[base + j * N_SLABS + sl, pl.ds(r, per_class, stride=d), :] = class_rows(r, j, sl)
        else:
            f = MAX_SINGLE_STRIDE
            for r0 in range(f):
                for j in range(len(refs)):
                    for sl in range(N_SLABS):
                        slab = j * N_SLABS + sl
                        half = (i * f + r0) * n_slabs + slab
                        for r1 in range(d // f):
                            half_ref[half, pl.ds(r1, per_class, stride=d // f), :] = class_rows(r1 * f + r0, j, sl)
                        stage_ref[base + slab, pl.ds(r0, ts // f, stride=f), :] = half_ref[half]
        return [jnp.concatenate([stage_ref[base + j * N_SLABS + sl] for sl in range(N_SLABS)], axis=1)
                for j in range(len(refs))]

    def mix_groups(i):
        o0, l0 = token_order((o0_ref, l0_ref), 0, i)
        o1, l1 = token_order((o1_ref, l1_ref), 1, i)
        o2, l2 = token_order((o2_ref, l2_ref), 2, i)
        m = jnp.maximum(jnp.maximum(l0, l1), l2)
        e0, e1, e2 = jnp.exp2(l0 - m), jnp.exp2(l1 - m), jnp.exp2(l2 - m)
        y_attn[i] = ((e0 * o0 + e1 * o1 + e2 * o2) / (e0 + e1 + e2)).astype(BF16)

    def branches(i):
        rows = slice(i * ts, (i + 1) * ts)
        gate_a = gate_ref[rows, :d_model].astype(F32)
        gate_b = gate_ref[rows, d_model:].astype(F32)
        merged[i] = (gate_a * jnp.dot(y_attn[i], wba_ref[...], preferred_element_type=F32)
                     + gate_b * jnp.dot(yg_ref[rows, :], wbg_ref[...], preferred_element_type=F32)).astype(BF16)

    def out_projection(i):
        y[i] = jnp.dot(merged[i], wo_ref[...], preferred_element_type=F32)

    def norms(i):
        x1[i] = x_ref[i * ts:(i + 1) * ts, :] + _rms(y[i]) * g_post_mix_ref[...]
        h[i] = (_rms(x1[i]) * g_pre_mlp_ref[...]).astype(BF16)
        acc[i] = jnp.zeros((ts, d_model), F32)

    def mlp_up(i, c):
        up = jnp.maximum(jnp.dot(h[i], w1_ref[:, c * ff_chunk:(c + 1) * ff_chunk],
                                 preferred_element_type=F32), 0.0)
        a[i] = (up * up).astype(BF16)

    def mlp_down(i, c):
        acc[i] = acc[i] + jnp.dot(a[i], w2_ref[c * ff_chunk:(c + 1) * ff_chunk, :], preferred_element_type=F32)

    def store(i):
        out_ref[i * ts:(i + 1) * ts, :] = x1[i] + _rms(acc[i]) * g_post_mlp_ref[...]

    mix_groups(0)
    for i in subs:
        branches(i)
        if i + 1 < n_sub:
            mix_groups(i + 1)
        out_projection(i)
    for i in subs:
        norms(i)
    for c in range(w1_ref.shape[1] // ff_chunk):
        for i in subs:
            mlp_up(i, c)
        for i in subs:
            mlp_down(i, c)
    for i in subs:
        store(i)


def _out_proj(x, attn, yg, gates, wba, wbg, wo, g_post_mix, g_pre_mlp, w1, w2, g_post_mlp, *, tm, ff_chunk):
    B, S, D = x.shape
    d_ff = w1.shape[1]
    n_sub = N_SUB_OUT_PROJ
    tok = lambda w: pl.BlockSpec((None, tm, w), lambda b, i: (b, i, 0))
    attn_args, attn_specs = [], []
    for d, parts in zip(DILATIONS, attn):
        attn_args += list(parts)
        attn_specs += [pl.BlockSpec((None, d, tm // d, GROUP_W), lambda b, i: (b, 0, i, 0))] * len(parts)
    return pl.pallas_call(
        functools.partial(_out_kernel, tm=tm, n_sub=n_sub, d_model=D, ff_chunk=ff_chunk),
        grid=(B, S // tm),
        in_specs=[tok(D)] + attn_specs + [
            tok(GMLP_WIDTH), tok(2 * D),
            _const_spec((GROUP_W, D)), _const_spec((GMLP_WIDTH, D)), _const_spec((D, D)),
            _const_spec((1, D)), _const_spec((1, D)),
            _const_spec((D, d_ff)), _const_spec((d_ff, D)), _const_spec((1, D)),
        ],
        out_specs=tok(D),
        out_shape=jax.ShapeDtypeStruct((B, S, D), F32),
        scratch_shapes=[
            pltpu.VMEM((n_sub * (N_GROUPS - 1) * 2 * N_SLABS, tm // n_sub, LANES), F32),
            pltpu.VMEM((n_sub * MAX_SINGLE_STRIDE * 2 * N_SLABS, tm // n_sub // MAX_SINGLE_STRIDE, LANES), F32)],
        compiler_params=pltpu.CompilerParams(
            dimension_semantics=("arbitrary", "arbitrary"), vmem_limit_bytes=VMEM_LIMIT_BYTES),
        name="out_proj_mlp",
    )(x, *attn_args, yg, gates, wba, wbg, wo, g_post_mix, g_pre_mlp, w1, w2, g_post_mlp)


def _in_proj_weights(w_in):
    d_model = w_in.shape[0]
    qk = w_in[:, :2 * ATTN_WIDTH].reshape(d_model, 2 * N_GROUPS, HEADS_PER_GROUP, 2, HALF)
    qk = qk.transpose(0, 1, 3, 2, 4).reshape(d_model, 2 * ATTN_WIDTH)
    return qk.astype(BF16), w_in[:, 2 * ATTN_WIDTH:].astype(BF16)


def _rope_tables(seq):
    inv_freq = ROPE_THETA ** (-jnp.arange(HALF, dtype=F32) / HALF)
    ang = jnp.arange(seq, dtype=F32)[:, None] * inv_freq[None, :]
    reps = LANES // HALF
    return jnp.tile(jnp.cos(ang), (1, reps)), jnp.tile(jnp.sin(ang), (1, reps))


def _attn_tiling(d, L):
    tq = min(L, ATTN_TOKENS_PER_STEP // d)
    rb = min(d, ATTN_TOKENS_PER_STEP // tq)
    assert tq % QBLK == 0 and d % rb == 0
    return rb, tq


def kernel(x, norm_pre_mix, w_in, w_spatial, b_spatial, ln_v_gain, ln_v_bias, w_branch_attn, w_branch_gmlp,
           w_out, norm_post_mix, norm_pre_mlp, w_mlp_in, w_mlp_out, norm_post_mlp):
    B, S, D = x.shape
    depth = w_in.shape[0]
    tm_in, tm_out = TM_IN_PROJ, TM_OUT_PROJ
    assert S % tm_in == 0 and S % tm_out == 0 and all(S % (d * QBLK) == 0 for d in DILATIONS)
    cos, sin = _rope_tables(S)
    row = lambda p: p.reshape(1, -1)
    for layer in range(depth):
        qkv0, qkv1, qkv2, yg, gates = _in_proj(
            x, row(norm_pre_mix[layer]), *_in_proj_weights(w_in[layer]), cos, sin,
            w_spatial[layer], b_spatial[layer].T, row(ln_v_gain[layer]), row(ln_v_bias[layer]), tm=tm_in)
        attn = []
        for qkv in (qkv0, qkv1, qkv2):
            rb, tq = _attn_tiling(qkv.shape[2], qkv.shape[3])
            attn.append(_attention(qkv, rb=rb, tq=tq))
        x = _out_proj(
            x, attn, yg, gates,
            w_branch_attn[layer].astype(BF16), w_branch_gmlp[layer].astype(BF16), w_out[layer].astype(BF16),
            row(norm_post_mix[layer]), row(norm_pre_mlp[layer]),
            w_mlp_in[layer].astype(BF16), w_mlp_out[layer].astype(BF16), row(norm_post_mlp[layer]),
            tm=tm_out, ff_chunk=FF_CHUNK)
    return x
```

```python
import functools

import numpy as np
import jax
import jax.numpy as jnp
from jax import lax
from jax.experimental import pallas as pl
from jax.experimental.pallas import tpu as pltpu

HEAD_DIM = 64
HALF = HEAD_DIM // 2
DIL_PATTERNS = ((128, 1), (512, 4), (2048, 16))
DILATIONS = tuple(d for _, d in DIL_PATTERNS)
N_GROUPS = len(DIL_PATTERNS)
HEADS_PER_GROUP = 4
GROUP_W = HEADS_PER_GROUP * HEAD_DIM
ATTN_WIDTH = N_GROUPS * GROUP_W
QBLK = 128
ROPE_THETA = 10000.0
CHUNK = 128
GMLP_GROUPS = 4
GMLP_GROUP_CH = 128
GMLP_WIDTH = GMLP_GROUPS * GMLP_GROUP_CH
EPS = 1e-6
NEG = -1e30
Q_SCALE = np.float32(HEAD_DIM ** -0.5 * np.log2(np.e))

LANES = 128
N_SLABS = GROUP_W // LANES
MAX_SINGLE_STRIDE = 4

BF16 = jnp.bfloat16
F32 = jnp.float32

VMEM_LIMIT_BYTES = 56 * 1024 * 1024
TM_IN_PROJ = 1024
TM_OUT_PROJ = 512
ATTN_TOKENS_PER_STEP = 2048
N_SUB_OUT_PROJ = 2
FF_CHUNK = 1024


def _rms(y):
    return y * lax.rsqrt(jnp.mean(y * y, axis=-1, keepdims=True) + EPS)


def _gelu_tanh(x):
    c = np.sqrt(2.0 / np.pi)
    inner = x * (np.float32(c) + np.float32(c * 0.044715) * (x * x))
    return x * (0.5 + 0.5 * jnp.tanh(inner))


def _sigmoid(x):
    return 1.0 / (1.0 + jnp.exp(-x))


def _const_spec(shape):
    return pl.BlockSpec(shape, lambda *_: (0,) * len(shape), pipeline_mode=pl.Buffered(1))


def _in_proj_kernel(x_ref, gain_ref, wqk_ref, wrest_ref, cos_ref, sin_ref, wsp_ref, bsp_ref, lng_ref, lnb_ref,
                    qkv0_ref, qkv1_ref, qkv2_ref, yg_ref, gate_ref, stage_ref, half_ref, *, tm, d_model):
    q0, k0, v0 = 0, ATTN_WIDTH, 2 * ATTN_WIDTH
    u0 = 3 * ATTN_WIDTH
    z0 = u0 + GMLP_WIDTH
    ga0 = z0 + GMLP_WIDTH

    h = (_rms(x_ref[...]) * gain_ref[...]).astype(BF16)
    cos = cos_ref[...]
    sin = sin_ref[...]

    def proj(c0, width):
        w = wqk_ref[:, c0:c0 + width] if c0 < v0 else wrest_ref[:, c0 - v0:c0 - v0 + width]
        return jnp.dot(h, w, preferred_element_type=F32)

    def rope(t):
        t1, t2 = t[:, :LANES], t[:, LANES:]
        return t1 * cos - t2 * sin, t2 * cos + t1 * sin

    def attention_group(g, qk, v):
        d, out_ref = DILATIONS[g], (qkv0_ref, qkv1_ref, qkv2_ref)[g]
        q = qk[:, q0 + g * GROUP_W:q0 + (g + 1) * GROUP_W] * Q_SCALE
        k = qk[:, k0 + g * GROUP_W:k0 + (g + 1) * GROUP_W]
        v = v[:, g * GROUP_W:(g + 1) * GROUP_W]
        slabs = rope(q) + rope(k) + (v[:, :LANES], v[:, LANES:])
        for s, val in enumerate(slabs):
            which, cols = s // N_SLABS, slice((s % N_SLABS) * LANES, (s % N_SLABS + 1) * LANES)
            if d == 1:
                out_ref[which, 0, :, cols] = val.astype(BF16)
                continue
            stage_ref[s] = val
            if d <= MAX_SINGLE_STRIDE:
                for r in range(d):
                    out_ref[which, r, :, cols] = stage_ref[s, pl.ds(r, tm // d, stride=d), :].astype(BF16)
            else:
                f = MAX_SINGLE_STRIDE
                for r0 in range(f):
                    half_ref[s * f + r0] = stage_ref[s, pl.ds(r0, tm // f, stride=f), :]
                    for r1 in range(d // f):
                        out_ref[which, r1 * f + r0, :, cols] = (
                            half_ref[s * f + r0, pl.ds(r1, tm // d, stride=d // f), :].astype(BF16))

    def gmlp_normalised(z):
        zc = z - jnp.mean(z, axis=-1, keepdims=True)
        var = jnp.mean(zc * zc, axis=-1, keepdims=True)
        return (zc * lax.rsqrt(var + EPS) * lng_ref[...] + lnb_ref[...]).astype(BF16)

    def gmlp_spatial_gating(u, zn):
        n_chunks = tm // CHUNK
        pos_out = lax.broadcasted_iota(jnp.int32, (CHUNK, CHUNK), 0)
        pos_in = lax.broadcasted_iota(jnp.int32, (CHUNK, CHUNK), 1)
        causal = pos_in <= pos_out
        for g in range(GMLP_GROUPS):
            cols = slice(g * LANES, (g + 1) * LANES)
            w = jnp.where(causal, wsp_ref[g], 0.0).astype(BF16)
            zcat = jnp.concatenate([zn[c * CHUNK:(c + 1) * CHUNK, cols] for c in range(n_chunks)], axis=1)
            sz = jnp.dot(w, zcat, preferred_element_type=F32) + bsp_ref[:, g:g + 1]
            for c in range(n_chunks):
                rows = slice(c * CHUNK, (c + 1) * CHUNK)
                yg_ref[rows, cols] = (u[rows, cols] * sz[:, c * CHUNK:(c + 1) * CHUNK]).astype(BF16)

    qk = proj(q0, 2 * ATTN_WIDTH)
    v = proj(v0, ATTN_WIDTH)
    for g in range(N_GROUPS):
        attention_group(g, qk, v)
    uz = _gelu_tanh(proj(u0, 2 * GMLP_WIDTH))
    zn = gmlp_normalised(uz[:, GMLP_WIDTH:])
    gate_ref[...] = _sigmoid(proj(ga0, 2 * d_model)).astype(BF16)
    gmlp_spatial_gating(uz[:, :GMLP_WIDTH], zn)


def _in_proj(x, gain, w_qk, w_rest, cos, sin, w_sp, b_sp_t, ln_g, ln_b, *, tm):
    B, S, D = x.shape
    kern = functools.partial(_in_proj_kernel, tm=tm, d_model=D)
    qkv_shapes = [jax.ShapeDtypeStruct((3, B, d, S // d, GROUP_W), BF16) for d in DILATIONS]
    qkv_specs = [pl.BlockSpec((3, None, d, tm // d, GROUP_W), lambda b, i: (0, b, 0, i, 0)) for d in DILATIONS]
    return pl.pallas_call(
        kern,
        grid=(B, S // tm),
        in_specs=[
            pl.BlockSpec((None, tm, D), lambda b, i: (b, i, 0)),
            _const_spec((1, D)),
            _const_spec(w_qk.shape),
            _const_spec(w_rest.shape),
            pl.BlockSpec((tm, LANES), lambda b, i: (i, 0)),
            pl.BlockSpec((tm, LANES), lambda b, i: (i, 0)),
            _const_spec((GMLP_GROUPS, CHUNK, CHUNK)),
            _const_spec((CHUNK, GMLP_GROUPS)),
            _const_spec((1, GMLP_WIDTH)),
            _const_spec((1, GMLP_WIDTH)),
        ],
        out_specs=qkv_specs + [
            pl.BlockSpec((None, tm, GMLP_WIDTH), lambda b, i: (b, i, 0)),
            pl.BlockSpec((None, tm, 2 * D), lambda b, i: (b, i, 0)),
        ],
        out_shape=qkv_shapes + [
            jax.ShapeDtypeStruct((B, S, GMLP_WIDTH), BF16),
            jax.ShapeDtypeStruct((B, S, 2 * D), BF16),
        ],
        scratch_shapes=[pltpu.VMEM((3 * N_SLABS, tm, LANES), F32),
                        pltpu.VMEM((3 * N_SLABS * MAX_SINGLE_STRIDE, tm // MAX_SINGLE_STRIDE, LANES), F32)],
        compiler_params=pltpu.CompilerParams(
            dimension_semantics=("arbitrary", "arbitrary"), vmem_limit_bytes=VMEM_LIMIT_BYTES),
        name="in_proj",
    )(x, gain, w_qk, w_rest, cos, sin, w_sp, b_sp_t, ln_g, ln_b)


def _attn_kernel(*refs, token_tile):
    in_refs, out_refs = refs[:5 * N_GROUPS], refs[5 * N_GROUPS:]
    first_tile = pl.program_id(1) == 0
    row = lax.broadcasted_iota(jnp.int32, (QBLK, 2 * QBLK), 0)
    col = lax.broadcasted_iota(jnp.int32, (QBLK, 2 * QBLK), 1)
    band = (col >= row) & (col - QBLK <= row)
    bias = jnp.where(band, 0.0, NEG).astype(F32)
    bias_first = jnp.where(band & ((col >= QBLK) | jnp.logical_not(first_tile)), 0.0, NEG).astype(F32)
    lane = lax.broadcasted_iota(jnp.int32, (QBLK, GROUP_W), 1)
    qk_head = (lane % LANES) // HALF
    heads_per_slab = LANES // HEAD_DIM
    low_head = lax.broadcasted_iota(jnp.int32, (QBLK, LANES), 1) < HEAD_DIM
    ones = jnp.ones((2 * QBLK, LANES), BF16)

    def attend(q, kcat, vcat, bias):
        qs = jnp.concatenate([jnp.where(qk_head == hh, q, jnp.zeros_like(q)) for hh in range(HEADS_PER_GROUP)],
                             axis=0)
        s = lax.dot_general(qs, kcat, (((1,), (1,)), ((), ())), preferred_element_type=F32)
        s = s.reshape(HEADS_PER_GROUP, QBLK, 2 * QBLK) + bias[None]
        m = jnp.max(s, axis=-1, keepdims=True)
        pb = jnp.exp2(s - m).astype(BF16)
        o, lse = [], []
        for t in range(N_SLABS):
            lo, hi = heads_per_slab * t, heads_per_slab * t + 1
            v_ones = jnp.concatenate([vcat[:, t * LANES:(t + 1) * LANES], ones], axis=1)
            pv = jnp.dot(pb[lo:hi + 1].reshape(heads_per_slab * QBLK, 2 * QBLK), v_ones,
                         preferred_element_type=F32)
            num = jnp.where(low_head, pv[:QBLK, :LANES], pv[QBLK:, :LANES])
            den = jnp.where(low_head, pv[:QBLK, LANES:], pv[QBLK:, LANES:])
            o.append(num * (1.0 / den))
            lse.append(jnp.where(low_head, m[lo], m[hi]) + jnp.log2(den))
        return jnp.concatenate(o, axis=1), jnp.concatenate(lse, axis=1)

    for g, d in enumerate(DILATIONS):
        q_ref, k_ref, v_ref, kp_ref, vp_ref = in_refs[5 * g:5 * g + 5]
        o_ref, lse_ref = out_refs[2 * g:2 * g + 2]
        for rr in range(d):
            for n in range(token_tile // d // QBLK):
                rows = slice(n * QBLK, (n + 1) * QBLK)
                if n == 0:
                    kcat = jnp.concatenate([kp_ref[rr], k_ref[rr, rows, :]], axis=0)
                    vcat = jnp.concatenate([vp_ref[rr], v_ref[rr, rows, :]], axis=0)
                else:
                    keys = slice((n - 1) * QBLK, (n + 1) * QBLK)
                    kcat, vcat = k_ref[rr, keys, :], v_ref[rr, keys, :]
                o, lse = attend(q_ref[rr, rows, :], kcat, vcat, bias_first if n == 0 else bias)
                o_ref[rr, rows, :] = o.astype(BF16)
                lse_ref[rr, rows, :] = lse


def _attention(qkvs, *, token_tile):
    B = qkvs[0].shape[1]
    S = qkvs[0].shape[2] * qkvs[0].shape[3]
    args, in_specs, out_specs, out_shapes = [], [], [], []
    for qkv in qkvs:
        d, L = qkv.shape[2], qkv.shape[3]
        tq = token_tile // d
        bpt = tq // QBLK
        assert tq % QBLK == 0 and L % tq == 0
        for which in (0, 1, 2):
            in_specs.append(pl.BlockSpec((None, None, d, tq, GROUP_W), lambda b, i, w=which: (w, b, 0, i, 0)))
        for which in (1, 2):
            in_specs.append(pl.BlockSpec((None, None, d, QBLK, GROUP_W),
                                         lambda b, i, w=which, bpt=bpt: (w, b, 0, jnp.maximum(i * bpt - 1, 0), 0)))
        args += [qkv] * 5
        out_specs += [pl.BlockSpec((None, d, tq, GROUP_W), lambda b, i: (b, 0, i, 0))] * 2
        out_shapes += [jax.ShapeDtypeStruct((B, d, L, GROUP_W), BF16),
                       jax.ShapeDtypeStruct((B, d, L, GROUP_W), F32)]
    outs = pl.pallas_call(
        functools.partial(_attn_kernel, token_tile=token_tile),
        grid=(B, S // token_tile),
        in_specs=in_specs,
        out_specs=out_specs,
        out_shape=out_shapes,
        compiler_params=pltpu.CompilerParams(
            dimension_semantics=("arbitrary", "arbitrary"), vmem_limit_bytes=VMEM_LIMIT_BYTES),
        name="attention",
    )(*args)
    return [outs[2 * g:2 * g + 2] for g in range(len(qkvs))]


def _out_kernel(x_ref, o0_ref, l0_ref, o1_ref, l1_ref, o2_ref, l2_ref, yg_ref, gate_ref,
                wba_ref, wbg_ref, wo_ref, g_post_mix_ref, g_pre_mlp_ref, w1_ref, w2_ref, g_post_mlp_ref,
                out_ref, stage_ref, half_ref, *, tm, n_sub, d_model, ff_chunk):
    ts = tm // n_sub
    subs = range(n_sub)
    y_attn, merged, y, x1, h, a, acc = ({} for _ in range(7))

    def token_order(refs, g, i):
        d = DILATIONS[g]
        if d == 1:
            return [ref[0, i * ts:(i + 1) * ts, :].astype(F32) for ref in refs]
        per_class = ts // d
        n_slabs = len(refs) * N_SLABS
        base = (i * (N_GROUPS - 1) + g - 1) * n_slabs

        def class_rows(r, j, sl):
            return refs[j][r, i * per_class:(i + 1) * per_class, sl * LANES:(sl + 1) * LANES].astype(F32)

        if d <= MAX_SINGLE_STRIDE:
            for r in range(d):
                for j in range(len(refs)):
                    for sl in range(N_SLABS):
                        stage_ref[base + j * N_SLABS + sl, pl.ds(r, per_class, stride=d), :] = class_rows(r, j, sl)
        else:
            f = MAX_SINGLE_STRIDE
            for r0 in range(f):
                for j in range(len(refs)):
                    for sl in range(N_SLABS):
                        slab = j * N_SLABS + sl
                        half = (i * f + r0) * n_slabs + slab
                        for r1 in range(d // f):
                            half_ref[half, pl.ds(r1, per_class, stride=d // f), :] = class_rows(r1 * f + r0, j, sl)
                        stage_ref[base + slab, pl.ds(r0, ts // f, stride=f), :] = half_ref[half]
        return [jnp.concatenate([stage_ref[base + j * N_SLABS + sl] for sl in range(N_SLABS)], axis=1)
                for j in range(len(refs))]

    def mix_groups(i):
        o0, l0 = token_order((o0_ref, l0_ref), 0, i)
        o1, l1 = token_order((o1_ref, l1_ref), 1, i)
        o2, l2 = token_order((o2_ref, l2_ref), 2, i)
        m = jnp.maximum(jnp.maximum(l0, l1), l2)
        e0, e1, e2 = jnp.exp2(l0 - m), jnp.exp2(l1 - m), jnp.exp2(l2 - m)
        y_attn[i] = ((e0 * o0 + e1 * o1 + e2 * o2) / (e0 + e1 + e2)).astype(BF16)

    def branches(i):
        rows = slice(i * ts, (i + 1) * ts)
        gate_a = gate_ref[rows, :d_model].astype(F32)
        gate_b = gate_ref[rows, d_model:].astype(F32)
        merged[i] = (gate_a * jnp.dot(y_attn[i], wba_ref[...], preferred_element_type=F32)
                     + gate_b * jnp.dot(yg_ref[rows, :], wbg_ref[...], preferred_element_type=F32)).astype(BF16)

    def out_projection(i):
        y[i] = jnp.dot(merged[i], wo_ref[...], preferred_element_type=F32)

    def norms(i):
        x1[i] = x_ref[i * ts:(i + 1) * ts, :] + _rms(y[i]) * g_post_mix_ref[...]
        h[i] = (_rms(x1[i]) * g_pre_mlp_ref[...]).astype(BF16)
        acc[i] = jnp.zeros((ts, d_model), F32)

    def mlp_up(i, c):
        up = jnp.maximum(jnp.dot(h[i], w1_ref[:, c * ff_chunk:(c + 1) * ff_chunk],
                                 preferred_element_type=F32), 0.0)
        a[i] = (up * up).astype(BF16)

    def mlp_down(i, c):
        acc[i] = acc[i] + jnp.dot(a[i], w2_ref[c * ff_chunk:(c + 1) * ff_chunk, :], preferred_element_type=F32)

    def store(i):
        out_ref[i * ts:(i + 1) * ts, :] = x1[i] + _rms(acc[i]) * g_post_mlp_ref[...]

    mix_groups(0)
    for i in subs:
        branches(i)
        if i + 1 < n_sub:
            mix_groups(i + 1)
        out_projection(i)
    for i in subs:
        norms(i)
    for c in range(w1_ref.shape[1] // ff_chunk):
        for i in subs:
            mlp_up(i, c)
        for i in subs:
            mlp_down(i, c)
    for i in subs:
        store(i)


def _out_proj(x, attn, yg, gates, wba, wbg, wo, g_post_mix, g_pre_mlp, w1, w2, g_post_mlp, *, tm, ff_chunk):
    B, S, D = x.shape
    d_ff = w1.shape[1]
    n_sub = N_SUB_OUT_PROJ
    tok = lambda w: pl.BlockSpec((None, tm, w), lambda b, i: (b, i, 0))
    attn_args, attn_specs = [], []
    for d, parts in zip(DILATIONS, attn):
        attn_args += list(parts)
        attn_specs += [pl.BlockSpec((None, d, tm // d, GROUP_W), lambda b, i: (b, 0, i, 0))] * len(parts)
    return pl.pallas_call(
        functools.partial(_out_kernel, tm=tm, n_sub=n_sub, d_model=D, ff_chunk=ff_chunk),
        grid=(B, S // tm),
        in_specs=[tok(D)] + attn_specs + [
            tok(GMLP_WIDTH), tok(2 * D),
            _const_spec((GROUP_W, D)), _const_spec((GMLP_WIDTH, D)), _const_spec((D, D)),
            _const_spec((1, D)), _const_spec((1, D)),
            _const_spec((D, d_ff)), _const_spec((d_ff, D)), _const_spec((1, D)),
        ],
        out_specs=tok(D),
        out_shape=jax.ShapeDtypeStruct((B, S, D), F32),
        scratch_shapes=[
            pltpu.VMEM((n_sub * (N_GROUPS - 1) * 2 * N_SLABS, tm // n_sub, LANES), F32),
            pltpu.VMEM((n_sub * MAX_SINGLE_STRIDE * 2 * N_SLABS, tm // n_sub // MAX_SINGLE_STRIDE, LANES), F32)],
        compiler_params=pltpu.CompilerParams(
            dimension_semantics=("arbitrary", "arbitrary"), vmem_limit_bytes=VMEM_LIMIT_BYTES),
        name="out_proj_mlp",
    )(x, *attn_args, yg, gates, wba, wbg, wo, g_post_mix, g_pre_mlp, w1, w2, g_post_mlp)


def _in_proj_weights(w_in):
    d_model = w_in.shape[0]
    qk = w_in[:, :2 * ATTN_WIDTH].reshape(d_model, 2 * N_GROUPS, HEADS_PER_GROUP, 2, HALF)
    qk = qk.transpose(0, 1, 3, 2, 4).reshape(d_model, 2 * ATTN_WIDTH)
    return qk.astype(BF16), w_in[:, 2 * ATTN_WIDTH:].astype(BF16)


def _rope_tables(seq):
    inv_freq = ROPE_THETA ** (-jnp.arange(HALF, dtype=F32) / HALF)
    ang = jnp.arange(seq, dtype=F32)[:, None] * inv_freq[None, :]
    reps = LANES // HALF
    return jnp.tile(jnp.cos(ang), (1, reps)), jnp.tile(jnp.sin(ang), (1, reps))


def kernel(x, norm_pre_mix, w_in, w_spatial, b_spatial, ln_v_gain, ln_v_bias, w_branch_attn, w_branch_gmlp,
           w_out, norm_post_mix, norm_pre_mlp, w_mlp_in, w_mlp_out, norm_post_mlp):
    B, S, D = x.shape
    depth = w_in.shape[0]
    tm_in, tm_out = TM_IN_PROJ, TM_OUT_PROJ
    assert S % tm_in == 0 and S % tm_out == 0 and all(S % (d * QBLK) == 0 for d in DILATIONS)
    cos, sin = _rope_tables(S)
    row = lambda p: p.reshape(1, -1)
    for layer in range(depth):
        qkv0, qkv1, qkv2, yg, gates = _in_proj(
            x, row(norm_pre_mix[layer]), *_in_proj_weights(w_in[layer]), cos, sin,
            w_spatial[layer], b_spatial[layer].T, row(ln_v_gain[layer]), row(ln_v_bias[layer]), tm=tm_in)
        attn = _attention((qkv0, qkv1, qkv2), token_tile=min(S, ATTN_TOKENS_PER_STEP))
        x = _out_proj(
            x, attn, yg, gates,
            w_branch_attn[layer].astype(BF16), w_branch_gmlp[layer].astype(BF16), w_out[layer].astype(BF16),
            row(norm_post_mix[layer]), row(norm_pre_mlp[layer]),
            w_mlp_in[layer].astype(BF16), w_mlp_out[layer].astype(BF16), row(norm_post_mlp[layer]),
            tm=tm_out, ff_chunk=FF_CHUNK)
    return x
```
